```python
import math
import jax
import jax.numpy as jnp
from jax import lax
import numpy as np

D_MODEL = 1024
BATCH = 2
SEQ = 8192
DEPTH = 4
DEC_BATCH = 128
DEC_SEQ = 1
PAST_LEN = 2048
PAGE_SIZE = 128

N_MIXERS = 3
A_HEADS = 8
A_HEAD_DIM = D_MODEL // A_HEADS
IDX_HEADS = 8
IDX_DIM = 64
TOPK_MAX = 256
A_IN = 3 * A_HEADS * A_HEAD_DIM + IDX_HEADS * IDX_DIM + IDX_DIM + IDX_HEADS
B_HEADS = 8
B_HEAD_DIM = D_MODEL // (2 * B_HEADS)
C_HEADS = 8
C_HEAD_DIM = D_MODEL // C_HEADS
D_FF = 4 * D_MODEL
ROPE_THETA = 10000.0
NORM_EPS = 1e-6
Q_BLOCK = 128
N_A = (DEPTH + 2) // 3
N_B = (DEPTH + 1) // 3
N_C = DEPTH // 3

kernel_name = 'hybrid_dsa_diff_stickbreak_decoder_step'


def rms_norm(x, g):
    xf = x.astype(jnp.float32)
    y = xf * lax.rsqrt(jnp.mean(xf * xf, axis=-1, keepdims=True) + NORM_EPS)
    return (y * g.astype(jnp.float32)).astype(x.dtype)


def rope(x, pos):
    half = x.shape[-1] // 2
    inv_freq = ROPE_THETA ** (-jnp.arange(half, dtype=jnp.float32) / half)
    ang = pos.astype(jnp.float32)[:, None] * inv_freq[None, :]
    cos, sin = jnp.cos(ang)[:, None, :], jnp.sin(ang)[:, None, :]
    xf = x.astype(jnp.float32)
    x1, x2 = xf[..., :half], xf[..., half:]
    return jnp.concatenate([x1 * cos - x2 * sin, x2 * cos + x1 * sin], axis=-1).astype(x.dtype)


def ada_modulation(c, w, b):
    m = jax.nn.silu(c) @ w + b
    return jnp.split(m[:, None, :], 6, axis=-1)


def modulate(h, shift, scale):
    return h * (1 + scale) + shift


def sq_relu_mlp(h, w1, w2):
    return jnp.square(jax.nn.relu(h @ w1)) @ w2


def gather_pages(pool, page_table):
    g = pool[page_table]
    return g.reshape(g.shape[0], g.shape[1] * g.shape[2], *g.shape[3:])


def take_rows(rows, idx):
    return jax.vmap(lambda r, i: r[i])(rows, idx)


def over_query_blocks(fn, *q_arrays):
    b, t = q_arrays[0].shape[:2]
    nb = t // Q_BLOCK
    blocks = tuple(jnp.moveaxis(a.reshape(b, nb, Q_BLOCK, *a.shape[2:]), 1, 0) for a in q_arrays)
    starts = jnp.arange(nb, dtype=jnp.int32) * Q_BLOCK
    out = lax.map(lambda args: fn(*args), (starts,) + blocks)
    out = jnp.moveaxis(out, 0, 1)
    return out.reshape(b, t, *out.shape[3:])


def dsa_project(h, w_in, q_norm, k_norm, idx_k_norm, pos):
    b, t, _ = h.shape
    hd = A_HEADS * A_HEAD_DIM
    cuts = [hd, 2 * hd, 3 * hd, 3 * hd + IDX_HEADS * IDX_DIM, 3 * hd + IDX_HEADS * IDX_DIM + IDX_DIM]
    q, k, v, qi, ki, wi = jnp.split(h @ w_in, cuts, axis=-1)
    q = rope(rms_norm(q.reshape(b, t, A_HEADS, A_HEAD_DIM), q_norm), pos)
    k = rope(rms_norm(k.reshape(b, t, A_HEADS, A_HEAD_DIM), k_norm), pos)
    v = v.reshape(b, t, A_HEADS, A_HEAD_DIM)
    qi = rope(qi.reshape(b, t, IDX_HEADS, IDX_DIM), pos)
    ki = rope(rms_norm(ki, idx_k_norm)[:, :, None, :], pos)[:, :, 0, :]
    wi = wi * IDX_HEADS ** -0.5
    return q, k, v, qi, ki, wi


def indexer_scores(qi, wi, ki):
    s = jnp.einsum('bthd,bsd->bths', qi, ki, preferred_element_type=jnp.float32) * IDX_DIM ** -0.5
    return jnp.einsum('bths,bth->bts', jax.nn.relu(s), wi.astype(jnp.float32))


def select_keys(scores, qpos, k_top):
    kpos = jnp.arange(scores.shape[-1], dtype=jnp.int32)
    admissible = kpos[None, :] <= qpos[:, None]
    _, idx = lax.top_k(jnp.where(admissible, scores, -jnp.inf), k_top)
    return idx, idx <= qpos[None, :, None]


def sparse_attend(q, k_sel, v_sel, valid):
    s = jnp.einsum('bthd,btkhd->bhtk', q, k_sel, preferred_element_type=jnp.float32) * A_HEAD_DIM ** -0.5
    p = jax.nn.softmax(jnp.where(valid[:, None], s, -jnp.inf), axis=-1)
    return jnp.einsum('bhtk,btkhd->bthd', p.astype(v_sel.dtype), v_sel)


def dsa_prompt(h, w_in, q_norm, k_norm, idx_k_norm, pos):
    b, t, _ = h.shape
    q, k, v, qi, ki, wi = dsa_project(h, w_in, q_norm, k_norm, idx_k_norm, pos)
    k_top = min(TOPK_MAX, t // 4)

    def block(start, qb, qib, wib):
        qpos = start + jnp.arange(Q_BLOCK, dtype=jnp.int32)
        idx, valid = select_keys(indexer_scores(qib, wib, ki), qpos, k_top)
        return sparse_attend(qb, take_rows(k, idx), take_rows(v, idx), valid)

    o = over_query_blocks(block, q, qi, wi)
    return o.reshape(b, t, A_HEADS * A_HEAD_DIM), k, v, ki


def dsa_sample(h, w_in, q_norm, k_norm, idx_k_norm, pos, pool_k, pool_v, pool_kidx, page_table):
    b, t, _ = h.shape
    q, k, v, qi, ki, wi = dsa_project(h, w_in, q_norm, k_norm, idx_k_norm, pos)
    past = page_table.shape[1] * PAGE_SIZE
    k_top = min(TOPK_MAX, (past + t) // 4)
    ki_all = jnp.concatenate([gather_pages(pool_kidx, page_table), ki], axis=1)
    idx, valid = select_keys(indexer_scores(qi, wi, ki_all), pos, k_top)
    from_past = (idx < past)[..., None, None]
    pidx = jnp.minimum(idx, past - 1)
    phys = take_rows(page_table, pidx // PAGE_SIZE)
    off = pidx % PAGE_SIZE
    nidx = jnp.clip(idx - past, 0, t - 1)
    k_sel = jnp.where(from_past, pool_k[phys, off], take_rows(k, nidx))
    v_sel = jnp.where(from_past, pool_v[phys, off], take_rows(v, nidx))
    o = sparse_attend(q, k_sel, v_sel, valid)
    return o.reshape(b, t, A_HEADS * A_HEAD_DIM), k, v, ki


def diff_project(h, w_in, q_norm, k_norm, pos):
    b, t, _ = h.shape
    q, k, v = jnp.split(h @ w_in, 3, axis=-1)
    shp = (b, t, 2 * B_HEADS, B_HEAD_DIM)
    q = rope(rms_norm(q.reshape(shp), q_norm), pos).reshape(b, t, B_HEADS, 2, B_HEAD_DIM)
    k = rope(rms_norm(k.reshape(shp), k_norm), pos).reshape(b, t, B_HEADS, 2, B_HEAD_DIM)
    return q, k, v.reshape(b, t, B_HEADS, 2 * B_HEAD_DIM)


def diff_lambda(lam_params, lam_init):
    lp = lam_params.astype(jnp.float32)
    return jnp.exp(jnp.sum(lp[0] * lp[1])) - jnp.exp(jnp.sum(lp[2] * lp[3])) + lam_init


def diff_scores(q, k):
    return jnp.einsum('bthcd,bshcd->bhcts', q, k, preferred_element_type=jnp.float32) * B_HEAD_DIM ** -0.5


def diff_weights(s, mask, lam):
    p = jax.nn.softmax(jnp.where(mask, s, -jnp.inf), axis=-1)
    return p[:, :, 0] - lam * p[:, :, 1]


def diff_out(o, subln, lam_init):
    b, t = o.shape[:2]
    return (rms_norm(o, subln) * (1.0 - lam_init)).reshape(b, t, B_HEADS * 2 * B_HEAD_DIM)


def diff_prompt(h, w_in, q_norm, k_norm, lam, lam_init, subln, pos):
    b, t, _ = h.shape
    q, k, v = diff_project(h, w_in, q_norm, k_norm, pos)
    kpos = jnp.arange(t, dtype=jnp.int32)

    def block(start, qb):
        qpos = start + jnp.arange(Q_BLOCK, dtype=jnp.int32)
        a = diff_weights(diff_scores(qb, k), kpos[None, :] <= qpos[:, None], lam)
        return jnp.einsum('bhts,bshe->bthe', a.astype(v.dtype), v)

    o = over_query_blocks(block, q)
    return diff_out(o, subln, lam_init), k.reshape(b, t, B_HEADS, 2 * B_HEAD_DIM), v


def diff_sample(h, w_in, q_norm, k_norm, lam, lam_init, subln, pos, pool_k, pool_v, page_table):
    b, t, _ = h.shape
    q, k, v = diff_project(h, w_in, q_norm, k_norm, pos)
    kp = gather_pages(pool_k, page_table)
    past = kp.shape[1]
    kp = kp.reshape(b, past, B_HEADS, 2, B_HEAD_DIM)
    vp = gather_pages(pool_v, page_table)
    kpos = jnp.arange(past + t, dtype=jnp.int32)
    s = jnp.concatenate([diff_scores(q, kp), diff_scores(q, k)], axis=-1)
    a = diff_weights(s, kpos[None, :] <= pos[:, None], lam).astype(v.dtype)
    o = jnp.einsum('bhts,bshe->bthe', a[..., :past], vp) + jnp.einsum('bhts,bshe->bthe', a[..., past:], v)
    return diff_out(o, subln, lam_init), k.reshape(b, t, B_HEADS, 2 * B_HEAD_DIM), v


def sb_project(h, w_in):
    b, t, _ = h.shape
    q, k, v = jnp.split(h @ w_in, 3, axis=-1)
    shp = (b, t, C_HEADS, C_HEAD_DIM)
    return q.reshape(shp), k.reshape(shp), v.reshape(shp)


def sb_scores(q, k):
    return jnp.einsum('bthd,bshd->bhts', q, k, preferred_element_type=jnp.float32) * C_HEAD_DIM ** -0.5


def sb_weights(z, mask):
    log_keep = jnp.where(mask, -jax.nn.softplus(z), 0.0)
    after = lax.cumsum(log_keep, axis=z.ndim - 1, reverse=True) - log_keep
    return jnp.where(mask, jnp.exp(after - jax.nn.softplus(-z)), 0.0)


def sb_prompt(h, w_in):
    b, t, _ = h.shape
    q, k, v = sb_project(h, w_in)
    kpos = jnp.arange(t, dtype=jnp.int32)

    def block(start, qb):
        qpos = start + jnp.arange(Q_BLOCK, dtype=jnp.int32)
        a = sb_weights(sb_scores(qb, k), kpos[None, :] < qpos[:, None])
        return jnp.einsum('bhts,bshd->bthd', a.astype(v.dtype), v)

    o = over_query_blocks(block, q)
    return o.reshape(b, t, C_HEADS * C_HEAD_DIM), k, v


def sb_sample(h, w_in, pos, pool_k, pool_v, page_table):
    b, t, _ = h.shape
    q, k, v = sb_project(h, w_in)
    kp = gather_pages(pool_k, page_table)
    vp = gather_pages(pool_v, page_table)
    past = kp.shape[1]
    kpos = jnp.arange(past + t, dtype=jnp.int32)
    z = jnp.concatenate([sb_scores(q, kp), sb_scores(q, k)], axis=-1)
    a = sb_weights(z, kpos[None, :] < pos[:, None]).astype(v.dtype)
    o = jnp.einsum('bhts,bshd->bthd', a[..., :past], vp) + jnp.einsum('bhts,bshd->bthd', a[..., past:], v)
    return o.reshape(b, t, C_HEADS * C_HEAD_DIM), k, v


def setup_inputs(seed: int = 0) -> dict:
    key = jax.random.key(seed)
    ks = jax.random.split(key, 48)
    counter = [0]

    def normal(shape, scale=1.0):
        counter[0] += 1
        return jax.random.normal(ks[counter[0]], shape, jnp.float32) * scale

    def gain(shape):
        return 1.0 + normal(shape, 0.1)

    d, dff = D_MODEL, D_FF
    n_pages = PAST_LEN // PAGE_SIZE
    n_used = DEC_BATCH * n_pages
    n_pool = n_used + max(1, n_used // 4)
    page_table = jax.random.permutation(ks[0], n_pool)[:n_used].reshape(DEC_BATCH, n_pages).astype(jnp.int32)
    return {
        'x_prompt': normal((BATCH, SEQ, d)),
        'x_sample': normal((DEC_BATCH, DEC_SEQ, d)),
        'cache_a_k': normal((N_A, n_pool, PAGE_SIZE, A_HEADS, A_HEAD_DIM)),
        'cache_a_v': normal((N_A, n_pool, PAGE_SIZE, A_HEADS, A_HEAD_DIM)),
        'cache_a_kidx': normal((N_A, n_pool, PAGE_SIZE, IDX_DIM)),
        'cache_b_k': normal((N_B, n_pool, PAGE_SIZE, B_HEADS, 2 * B_HEAD_DIM)),
        'cache_b_v': normal((N_B, n_pool, PAGE_SIZE, B_HEADS, 2 * B_HEAD_DIM)),
        'cache_c_k': normal((N_C, n_pool, PAGE_SIZE, C_HEADS, C_HEAD_DIM)),
        'cache_c_v': normal((N_C, n_pool, PAGE_SIZE, C_HEADS, C_HEAD_DIM)),
        'page_table': page_table,
        'c_prompt': normal((BATCH, d)),
        'c_sample': normal((DEC_BATCH, d)),
        'ada_w': normal((DEPTH, d, 6 * d), 0.5 * d ** -0.5),
        'ada_b': normal((DEPTH, 6 * d), 0.1),
        'norm_mix': gain((DEPTH, d)),
        'norm_ffn': gain((DEPTH, d)),
        'ffn_w1': normal((DEPTH, d, dff), d ** -0.5),
        'ffn_w2': normal((DEPTH, dff, d), dff ** -0.5),
        'a_w_in': normal((N_A, d, A_IN), d ** -0.5),
        'a_q_norm': gain((N_A, A_HEAD_DIM)),
        'a_k_norm': gain((N_A, A_HEAD_DIM)),
        'a_idx_k_norm': gain((N_A, IDX_DIM)),
        'a_w_out': normal((N_A, A_HEADS * A_HEAD_DIM, d), (A_HEADS * A_HEAD_DIM) ** -0.5),
        'b_w_in': normal((N_B, d, 3 * d), d ** -0.5),
        'b_q_norm': gain((N_B, B_HEAD_DIM)),
        'b_k_norm': gain((N_B, B_HEAD_DIM)),
        'b_lambda': normal((N_B, 4, B_HEAD_DIM), 0.1),
        'b_subln': gain((N_B, 2 * B_HEAD_DIM)),
        'b_w_out': normal((N_B, d, d), d ** -0.5),
        'c_w_in': normal((N_C, d, 3 * d), d ** -0.5),
        'c_w_out': normal((N_C, d, d), d ** -0.5),
    }


def reference(x_prompt, x_sample, cache_a_k, cache_a_v, cache_a_kidx, cache_b_k, cache_b_v,
              cache_c_k, cache_c_v, page_table, c_prompt, c_sample, ada_w, ada_b, norm_mix, norm_ffn,
              ffn_w1, ffn_w2, a_w_in, a_q_norm, a_k_norm, a_idx_k_norm, a_w_out, b_w_in, b_q_norm,
              b_k_norm, b_lambda, b_subln, b_w_out, c_w_in, c_w_out):
    pos_p = jnp.arange(x_prompt.shape[1], dtype=jnp.int32)
    past = page_table.shape[1] * PAGE_SIZE
    pos_s = past + jnp.arange(x_sample.shape[1], dtype=jnp.int32)
    xp, xs = x_prompt, x_sample
    rows = {name: [] for name in ('a_k', 'a_v', 'a_kidx', 'b_k', 'b_v', 'c_k', 'c_v')}
    for i in range(DEPTH):
        kind, j = i % N_MIXERS, i // N_MIXERS
        mod_p = ada_modulation(c_prompt, ada_w[i], ada_b[i])
        mod_s = ada_modulation(c_sample, ada_w[i], ada_b[i])
        hp = modulate(rms_norm(xp, norm_mix[i]), mod_p[0], mod_p[1])
        hs = modulate(rms_norm(xs, norm_mix[i]), mod_s[0], mod_s[1])
        if kind == 0:
            op, kp, vp, kip = dsa_prompt(hp, a_w_in[j], a_q_norm[j], a_k_norm[j], a_idx_k_norm[j], pos_p)
            os_, ks_, vs_, kis = dsa_sample(hs, a_w_in[j], a_q_norm[j], a_k_norm[j], a_idx_k_norm[j], pos_s,
                                            cache_a_k[j], cache_a_v[j], cache_a_kidx[j], page_table)
            rows['a_k'].append((kp, ks_))
            rows['a_v'].append((vp, vs_))
            rows['a_kidx'].append((kip, kis))
            w_out = a_w_out[j]
        elif kind == 1:
            lam_init = 0.8 - 0.6 * math.exp(-0.3 * i)
            lam = diff_lambda(b_lambda[j], lam_init)
            op, kp, vp = diff_prompt(hp, b_w_in[j], b_q_norm[j], b_k_norm[j], lam, lam_init, b_subln[j], pos_p)
            os_, ks_, vs_ = diff_sample(hs, b_w_in[j], b_q_norm[j], b_k_norm[j], lam, lam_init, b_subln[j], pos_s,
                                        cache_b_k[j], cache_b_v[j], page_table)
            rows['b_k'].append((kp, ks_))
            rows['b_v'].append((vp, vs_))
            w_out = b_w_out[j]
        else:
            op, kp, vp = sb_prompt(hp, c_w_in[j])
            os_, ks_, vs_ = sb_sample(hs, c_w_in[j], pos_s, cache_c_k[j], cache_c_v[j], page_table)
            rows['c_k'].append((kp, ks_))
            rows['c_v'].append((vp, vs_))
            w_out = c_w_out[j]
        xp = xp + mod_p[2] * (op @ w_out)
        xs = xs + mod_s[2] * (os_ @ w_out)
        xp = xp + mod_p[5] * sq_relu_mlp(modulate(rms_norm(xp, norm_ffn[i]), mod_p[3], mod_p[4]), ffn_w1[i], ffn_w2[i])
        xs = xs + mod_s[5] * sq_relu_mlp(modulate(rms_norm(xs, norm_ffn[i]), mod_s[3], mod_s[4]), ffn_w1[i], ffn_w2[i])

    a_k_prompt = jnp.stack([r[0] for r in rows['a_k']])
    a_v_prompt = jnp.stack([r[0] for r in rows['a_v']])
    a_kidx_prompt = jnp.stack([r[0] for r in rows['a_kidx']])
    b_k_prompt = jnp.stack([r[0] for r in rows['b_k']])
    b_v_prompt = jnp.stack([r[0] for r in rows['b_v']])
    c_k_prompt = jnp.stack([r[0] for r in rows['c_k']])
    c_v_prompt = jnp.stack([r[0] for r in rows['c_v']])
    a_k_sample = jnp.stack([r[1] for r in rows['a_k']])
    a_v_sample = jnp.stack([r[1] for r in rows['a_v']])
    a_kidx_sample = jnp.stack([r[1] for r in rows['a_kidx']])
    b_k_sample = jnp.stack([r[1] for r in rows['b_k']])
    b_v_sample = jnp.stack([r[1] for r in rows['b_v']])
    c_k_sample = jnp.stack([r[1] for r in rows['c_k']])
    c_v_sample = jnp.stack([r[1] for r in rows['c_v']])
    return (xp, xs, a_k_prompt, a_v_prompt, a_kidx_prompt, b_k_prompt, b_v_prompt, c_k_prompt, c_v_prompt,
            a_k_sample, a_v_sample, a_kidx_sample, b_k_sample, b_v_sample, c_k_sample, c_v_sample)
```

```python
import functools
import math

import jax
import jax.numpy as jnp
from jax import lax
from jax.experimental import pallas as pl
from jax.experimental.pallas import tpu as pltpu

F32 = jnp.float32
BF16 = jnp.bfloat16
I32 = jnp.int32

D_MODEL = 1024
N_HEADS = 8
HEAD_DIM = 128
IDX_HEADS = 8
IDX_DIM = 64
TOPK_MAX = 256
PAGE_SIZE = 128
ROPE_THETA = 10000.0
NORM_EPS = 1e-6
N_MIXERS = 3

LANES = 128
NEG = -1e30
INT_MIN = -2147483648
VMEM_LIMIT = 56 * 1024 * 1024
NT_DIMS = (((1,), (1,)), ((), ()))


def _cparams(sem):
    return pltpu.CompilerParams(dimension_semantics=sem, vmem_limit_bytes=VMEM_LIMIT)


def _split_bf16(x):
    hi = x.astype(BF16)
    lo = (x - hi.astype(F32)).astype(BF16)
    return hi, lo


def _dot(a, b):
    return jnp.dot(a, b, preferred_element_type=F32)


def _dot_nt(a, b):
    return lax.dot_general(a, b, NT_DIMS, preferred_element_type=F32)


def _norm_mod(x, g, sc, sh):
    ms = jnp.mean(x * x, axis=-1, keepdims=True)
    return (x * lax.rsqrt(ms + NORM_EPS) * g) * (1.0 + sc) + sh


def _mod_spec(rm, tm, d):
    if rm == 1:
        return pl.BlockSpec((None, 1, d), lambda g, i, j: (g, 0, 0))
    return pl.BlockSpec((None, tm, d), lambda g, i, j: (g, i, 0))


def _ada_body(c_ref, w_ref, b_ref, o_ref):
    c = c_ref[...]
    s = c * jax.nn.sigmoid(c)
    s_hi, s_lo = _split_bf16(s)
    w_hi, w_lo = _split_bf16(w_ref[...])
    o_ref[...] = _dot(s_hi, w_hi) + _dot(s_lo, w_hi) + _dot(s_hi, w_lo) + b_ref[...]


def ada_modulation_all(c_all, ada_w, ada_b):
    depth, d, n = ada_w.shape
    r = c_all.shape[0]
    tn = 1536
    return pl.pallas_call(
        _ada_body,
        grid=(depth, n // tn),
        in_specs=[
            pl.BlockSpec((r, d), lambda i, j: (0, 0)),
            pl.BlockSpec((None, d, tn), lambda i, j: (i, 0, j)),
            pl.BlockSpec((None, 1, tn), lambda i, j: (i, 0, j)),
        ],
        out_specs=pl.BlockSpec((None, r, tn), lambda i, j: (i, 0, j)),
        out_shape=jax.ShapeDtypeStruct((depth, r, n), F32),
        compiler_params=_cparams(("arbitrary", "arbitrary")),
        name="ada_modulation",
    )(c_all, ada_w, ada_b.reshape(depth, 1, n))


def _rope_tab_body(invf_ref, sgn_ref, cos_ref, sin_ref, *, start, tr):
    i = pl.program_id(0)
    pos = (start + i * tr + lax.broadcasted_iota(I32, (tr, LANES), 0)).astype(F32)
    ang = pos * invf_ref[...]
    cos_ref[...] = jnp.cos(ang)
    sin_ref[...] = jnp.sin(ang) * sgn_ref[...]


def rope_tables(start, rows, half):
    inv_freq = ROPE_THETA ** (-jnp.arange(half, dtype=F32) / half)
    reps = LANES // (2 * half)
    invf = jnp.tile(jnp.concatenate([inv_freq, inv_freq]), reps).reshape(1, LANES)
    sgn = jnp.tile(jnp.concatenate([-jnp.ones((half,), F32), jnp.ones((half,), F32)]), reps).reshape(1, LANES)
    tr = min(rows, 512)
    return pl.pallas_call(
        functools.partial(_rope_tab_body, start=start, tr=tr),
        grid=(rows // tr,),
        in_specs=[pl.BlockSpec((1, LANES), lambda i: (0, 0)), pl.BlockSpec((1, LANES), lambda i: (0, 0))],
        out_specs=[pl.BlockSpec((tr, LANES), lambda i: (i, 0)), pl.BlockSpec((tr, LANES), lambda i: (i, 0))],
        out_shape=[jax.ShapeDtypeStruct((rows, LANES), F32)] * 2,
        compiler_params=_cparams(("arbitrary",)),
        name="rope_tables",
    )(invf, sgn)


def _proj_body(x_ref, sh_ref, sc_ref, g_ref, w_ref, o_ref, hn_ref):
    @pl.when(pl.program_id(2) == 0)
    def _():
        hn_ref[...] = _norm_mod(x_ref[...], g_ref[...], sc_ref[...], sh_ref[...]).astype(BF16)

    o_ref[...] = _dot(hn_ref[...], w_ref[...])


def _proj3_body(x_ref, sh_ref, sc_ref, g_ref, whi_ref, wlo_ref, o_ref, hhi_ref, hlo_ref):
    @pl.when(pl.program_id(2) == 0)
    def _():
        hi, lo = _split_bf16(_norm_mod(x_ref[...], g_ref[...], sc_ref[...], sh_ref[...]))
        hhi_ref[...] = hi
        hlo_ref[...] = lo

    o_ref[...] = (_dot(hhi_ref[...], whi_ref[...]) + _dot(hlo_ref[...], whi_ref[...])
                  + _dot(hhi_ref[...], wlo_ref[...]))


def proj(x, g, sh, sc, ws, *, tm, tn, name):
    grp, r, d = x.shape
    n = ws[0].shape[1]
    tm = min(tm, r)
    precise = len(ws) == 2
    w_spec = pl.BlockSpec((d, tn), lambda b, i, j: (0, j))
    return pl.pallas_call(
        _proj3_body if precise else _proj_body,
        grid=(grp, r // tm, n // tn),
        in_specs=[
            pl.BlockSpec((None, tm, d), lambda b, i, j: (b, i, 0)),
            _mod_spec(sh.shape[1], tm, d),
            _mod_spec(sc.shape[1], tm, d),
            pl.BlockSpec((1, d), lambda b, i, j: (0, 0)),
        ] + [w_spec] * len(ws),
        out_specs=pl.BlockSpec((None, tm, tn), lambda b, i, j: (b, i, j)),
        out_shape=jax.ShapeDtypeStruct((grp, r, n), F32),
        scratch_shapes=[pltpu.VMEM((tm, d), BF16)] * len(ws),
        compiler_params=_cparams(("arbitrary", "arbitrary", "arbitrary")),
        name=name,
    )(x, sh, sc, g, *ws)


def _oproj_body(o_ref, w_ref, x_ref, gt_ref, y_ref):
    y_ref[...] = x_ref[...] + gt_ref[...] * _dot(o_ref[...], w_ref[...])


def out_proj_residual(o, w, x, gate, *, tm):
    grp, r, d = x.shape
    tm = min(tm, r)
    return pl.pallas_call(
        _oproj_body,
        grid=(grp, r // tm, 1),
        in_specs=[
            pl.BlockSpec((None, tm, d), lambda b, i, j: (b, i, 0)),
            pl.BlockSpec((d, d), lambda b, i, j: (0, 0)),
            pl.BlockSpec((None, tm, d), lambda b, i, j: (b, i, 0)),
            _mod_spec(gate.shape[1], tm, d),
        ],
        out_specs=pl.BlockSpec((None, tm, d), lambda b, i, j: (b, i, 0)),
        out_shape=jax.ShapeDtypeStruct((grp, r, d), F32),
        compiler_params=_cparams(("arbitrary", "arbitrary", "arbitrary")),
        name="out_proj_residual",
    )(o, w, x, gate)


def _mlp_body(x_ref, sh_ref, sc_ref, gt_ref, g_ref, w1_ref, w2_ref, y_ref, hn_ref, acc_ref, *, nf):
    f = pl.program_id(2)

    @pl.when(f == 0)
    def _():
        hn_ref[...] = _norm_mod(x_ref[...], g_ref[...], sc_ref[...], sh_ref[...]).astype(BF16)
        acc_ref[...] = jnp.zeros_like(acc_ref)

    h = jnp.maximum(_dot(hn_ref[...], w1_ref[...]), 0.0)
    acc_ref[...] += _dot((h * h).astype(BF16), w2_ref[...])

    @pl.when(f == nf - 1)
    def _():
        y_ref[...] = x_ref[...] + gt_ref[...] * acc_ref[...]


def mlp_residual(x, g, sh, sc, gate, w1, w2, *, tm, tf):
    grp, r, d = x.shape
    dff = w1.shape[1]
    tm = min(tm, r)
    nf = dff // tf
    return pl.pallas_call(
        functools.partial(_mlp_body, nf=nf),
        grid=(grp, r // tm, nf),
        in_specs=[
            pl.BlockSpec((None, tm, d), lambda b, i, f: (b, i, 0)),
            _mod_spec(sh.shape[1], tm, d),
            _mod_spec(sc.shape[1], tm, d),
            _mod_spec(gate.shape[1], tm, d),
            pl.BlockSpec((1, d), lambda b, i, f: (0, 0)),
            pl.BlockSpec((d, tf), lambda b, i, f: (0, f)),
            pl.BlockSpec((tf, d), lambda b, i, f: (f, 0)),
        ],
        out_specs=pl.BlockSpec((None, tm, d), lambda b, i, f: (b, i, 0)),
        out_shape=jax.ShapeDtypeStruct((grp, r, d), F32),
        scratch_shapes=[pltpu.VMEM((tm, d), BF16), pltpu.VMEM((tm, d), F32)],
        compiler_params=_cparams(("arbitrary", "arbitrary", "arbitrary")),
        name="mlp_residual",
    )(x, sh, sc, gate, g, w1, w2)


def _rms_lanes(x, gain, width):
    sq = x * x
    if width == LANES:
        ms = jnp.mean(sq, axis=-1, keepdims=True)
    else:
        low = lax.broadcasted_iota(I32, (1, LANES), 1) < width
        s_lo = jnp.sum(jnp.where(low, sq, 0.0), axis=-1, keepdims=True)
        s_hi = jnp.sum(jnp.where(low, 0.0, sq), axis=-1, keepdims=True)
        ms = jnp.where(low, s_lo, s_hi) * (1.0 / width)
    return x * lax.rsqrt(ms + NORM_EPS) * gain


def _rope128(x, cos, sin_s):
    return x * cos + pltpu.roll(x, 64, 1) * sin_s


def _rope64(x, cos, sin_s):
    first = (lax.broadcasted_iota(I32, (1, LANES), 1) % 64) < 32
    partner = jnp.where(first, pltpu.roll(x, 96, 1), pltpu.roll(x, 32, 1))
    return x * cos + partner * sin_s


def _post_a_body(raw_ref, idx_ref, c128_ref, s128_ref, c64_ref, s64_ref, qn_ref, kn_ref, ikn_ref,
                 q_ref, kf_ref, kb_ref, vb_ref, q3_ref, kif_ref, k3_ref, wi_ref):
    c128, s128 = c128_ref[...], s128_ref[...]
    c64, s64 = c64_ref[...], s64_ref[...]
    d = N_HEADS * HEAD_DIM
    for h in range(N_HEADS):
        sl = slice(h * HEAD_DIM, (h + 1) * HEAD_DIM)
        q = _rope128(_rms_lanes(raw_ref[:, sl], qn_ref[...], LANES), c128, s128)
        q_ref[:, sl] = (q * HEAD_DIM ** -0.5).astype(BF16)
        k = _rope128(_rms_lanes(raw_ref[:, d + h * HEAD_DIM:d + (h + 1) * HEAD_DIM], kn_ref[...], LANES), c128, s128)
        kf_ref[:, sl] = k
        kb_ref[:, sl] = k.astype(BF16)
        qi = _rope64(idx_ref[:, sl], c64, s64) * IDX_DIM ** -0.5
        qi_hi, qi_lo = _split_bf16(qi)
        q3_ref[h, :, 0:LANES] = (qi_hi.astype(F32) + pltpu.roll(qi_lo.astype(F32), 64, 1)).astype(BF16)
        q3_ref[h, :, LANES:2 * LANES] = qi_hi
    vb_ref[...] = raw_ref[:, 2 * d:3 * d].astype(BF16)
    ki_raw = idx_ref[:, d:d + LANES]
    ms = jnp.sum(ki_raw * ki_raw, axis=-1, keepdims=True) * (1.0 / IDX_DIM)
    ki = _rope64(ki_raw * lax.rsqrt(ms + NORM_EPS) * ikn_ref[...], c64, s64)
    kif_ref[...] = ki[:, :IDX_DIM]
    ki_hi, ki_lo = _split_bf16(ki)
    k3_ref[:, 0:LANES] = (ki_hi.astype(F32) + pltpu.roll(ki_hi.astype(F32), 64, 1)).astype(BF16)
    k3_ref[:, LANES:2 * LANES] = ki_lo
    wi_ref[...] = idx_ref[:, d + LANES:d + LANES + IDX_HEADS] * IDX_HEADS ** -0.5


def post_a(raw, idx, tabs128, tabs64, qn, kn, ikn, *, tm):
    grp, r, _ = raw.shape
    tm = min(tm, r)
    d = N_HEADS * HEAD_DIM
    rt = tabs128[0].shape[0]
    tab_spec = (pl.BlockSpec((1, LANES), lambda b, i: (0, 0)) if rt == 1
                else pl.BlockSpec((tm, LANES), lambda b, i: (i, 0)))
    vec_spec = pl.BlockSpec((1, LANES), lambda b, i: (0, 0))
    row = lambda w: pl.BlockSpec((None, tm, w), lambda b, i: (b, i, 0))
    return pl.pallas_call(
        _post_a_body,
        grid=(grp, r // tm),
        in_specs=[row(3 * d), row(idx.shape[2]), tab_spec, tab_spec, tab_spec, tab_spec,
                  vec_spec, vec_spec, vec_spec],
        out_specs=[row(d), row(d), row(d), row(d),
                   pl.BlockSpec((None, IDX_HEADS, tm, 2 * LANES), lambda b, i: (b, 0, i, 0)),
                   row(IDX_DIM), row(2 * LANES), row(IDX_HEADS)],
        out_shape=[jax.ShapeDtypeStruct((grp, r, d), BF16), jax.ShapeDtypeStruct((grp, r, d), F32),
                   jax.ShapeDtypeStruct((grp, r, d), BF16), jax.ShapeDtypeStruct((grp, r, d), BF16),
                   jax.ShapeDtypeStruct((grp, IDX_HEADS, r, 2 * LANES), BF16),
                   jax.ShapeDtypeStruct((grp, r, IDX_DIM), F32),
                   jax.ShapeDtypeStruct((grp, r, 2 * LANES), BF16),
                   jax.ShapeDtypeStruct((grp, r, IDX_HEADS), F32)],
        compiler_params=_cparams(("arbitrary", "arbitrary")),
        name="post_dsa",
    )(raw, idx, tabs128[0], tabs128[1], tabs64[0], tabs64[1], qn, kn, ikn)


def _post_b_body(raw_ref, c64_ref, s64_ref, qn_ref, kn_ref, q2_ref, kf_ref, kb_ref, vb_ref):
    c64, s64 = c64_ref[...], s64_ref[...]
    d = N_HEADS * HEAD_DIM
    low = lax.broadcasted_iota(I32, (1, LANES), 1) < 64
    for h in range(N_HEADS):
        sl = slice(h * HEAD_DIM, (h + 1) * HEAD_DIM)
        q = _rope64(_rms_lanes(raw_ref[:, sl], qn_ref[...], 64), c64, s64) * (HEAD_DIM // 2) ** -0.5
        q2_ref[0, :, sl] = jnp.where(low, q, 0.0).astype(BF16)
        q2_ref[1, :, sl] = jnp.where(low, 0.0, q).astype(BF16)
        k = _rope64(_rms_lanes(raw_ref[:, d + h * HEAD_DIM:d + (h + 1) * HEAD_DIM], kn_ref[...], 64), c64, s64)
        kf_ref[:, sl] = k
        kb_ref[:, sl] = k.astype(BF16)
    vb_ref[...] = raw_ref[:, 2 * d:3 * d].astype(BF16)


def post_b(raw, tabs64, qn, kn, *, tm):
    grp, r, _ = raw.shape
    tm = min(tm, r)
    d = N_HEADS * HEAD_DIM
    rt = tabs64[0].shape[0]
    tab_spec = (pl.BlockSpec((1, LANES), lambda b, i: (0, 0)) if rt == 1
                else pl.BlockSpec((tm, LANES), lambda b, i: (i, 0)))
    vec_spec = pl.BlockSpec((1, LANES), lambda b, i: (0, 0))
    row = lambda w: pl.BlockSpec((None, tm, w), lambda b, i: (b, i, 0))
    return pl.pallas_call(
        _post_b_body,
        grid=(grp, r // tm),
        in_specs=[row(3 * d), tab_spec, tab_spec, vec_spec, vec_spec],
        out_specs=[pl.BlockSpec((None, 2, tm, d), lambda b, i: (b, 0, i, 0)), row(d), row(d), row(d)],
        out_shape=[jax.ShapeDtypeStruct((grp, 2, r, d), BF16), jax.ShapeDtypeStruct((grp, r, d), F32),
                   jax.ShapeDtypeStruct((grp, r, d), BF16), jax.ShapeDtypeStruct((grp, r, d), BF16)],
        compiler_params=_cparams(("arbitrary", "arbitrary")),
        name="post_diff",
    )(raw, tabs64[0], tabs64[1], qn, kn)


def _post_c_body(raw_ref, q_ref, kb_ref, vb_ref):
    d = N_HEADS * HEAD_DIM
    q_ref[...] = (raw_ref[:, 0:d] * HEAD_DIM ** -0.5).astype(BF16)
    kb_ref[...] = raw_ref[:, d:2 * d].astype(BF16)
    vb_ref[...] = raw_ref[:, 2 * d:3 * d].astype(BF16)


def post_c(raw, *, tm):
    grp, r, _ = raw.shape
    tm = min(tm, r)
    d = N_HEADS * HEAD_DIM
    row = lambda w: pl.BlockSpec((None, tm, w), lambda b, i: (b, i, 0))
    return pl.pallas_call(
        _post_c_body,
        grid=(grp, r // tm),
        in_specs=[row(3 * d)],
        out_specs=[row(d), row(d), row(d)],
        out_shape=[jax.ShapeDtypeStruct((grp, r, d), BF16)] * 3,
        compiler_params=_cparams(("arbitrary", "arbitrary")),
        name="post_sb",
    )(raw)


def _sort_key(score):
    bits = pltpu.bitcast(score, I32)
    bits = jnp.where(bits == INT_MIN, 0, bits)
    return jnp.where(bits < 0, bits ^ 0x7FFFFFFF, bits)


def _lane_total(cnt):
    tot = jnp.sum(cnt.astype(F32), axis=1, keepdims=True)
    return jnp.broadcast_to(tot, cnt.shape).astype(I32)


def _kth_largest(count_ge, rows, ktop):
    def step(i, cur):
        trial = cur + lax.shift_left(jnp.int32(1), 31 - i)
        return jnp.where(count_ge(trial) >= ktop, trial, cur)

    return lax.fori_loop(0, 32, step, jnp.full((rows, LANES), INT_MIN, I32))


def _tie_cutoff(count_eq_below, need, rows, nbits):
    def step(i, cur):
        trial = cur + lax.shift_left(jnp.int32(1), nbits - 1 - i)
        return jnp.where(count_eq_below(trial) < need, trial, cur)

    return lax.fori_loop(0, nbits, step, jnp.zeros((rows, LANES), I32))


def _dsa_select_body(q3_ref, wi_ref, k3_ref, bias_ref, key_ref, cut_ref, *, tq, ck, t_len, ktop):
    qb = pl.program_id(1)
    row0 = qb * tq
    nch = (row0 + tq - 1) // ck + 1
    nsub = ck // LANES
    q3 = q3_ref[...].reshape(IDX_HEADS * tq, 2 * LANES)
    wi = wi_ref[...]
    rows = row0 + lax.broadcasted_iota(I32, (tq, ck), 0)
    lane_cols = lax.broadcasted_iota(I32, (tq, ck), 1)

    def score_chunk(c, carry):
        c0 = pl.multiple_of(c * ck, ck)
        s = _dot_nt(q3, k3_ref[pl.ds(c0, ck), :])
        acc = jnp.zeros((tq, ck), F32)
        for h in range(IDX_HEADS):
            acc = acc + jnp.maximum(s[h * tq:(h + 1) * tq], 0.0) * wi[:, h:h + 1]
        key_ref[c] = jnp.where(c0 + lane_cols <= rows, _sort_key(acc), INT_MIN)
        return carry

    lax.fori_loop(0, nch, score_chunk, 0)

    def count(ind):
        def body(c, cnt):
            m = ind(key_ref[c], c * ck + lane_cols)
            for j in range(nsub):
                cnt = cnt + m[:, j * LANES:(j + 1) * LANES]
            return cnt
        return _lane_total(lax.fori_loop(0, nch, body, jnp.zeros((tq, LANES), I32)))

    def wide(x):
        return jnp.concatenate([x] * nsub, axis=1)

    thr = _kth_largest(lambda trial: count(lambda kc, cols: jnp.where(kc >= wide(trial), 1, 0)), tq, ktop)
    thr_w = wide(thr)
    need = ktop - count(lambda kc, cols: jnp.where(kc > thr_w, 1, 0))
    n_eq = count(lambda kc, cols: jnp.where(kc == thr_w, 1, 0))
    cut_ref[...] = jnp.full((tq, LANES), t_len, I32)

    @pl.when(jnp.max(n_eq - need) > 0)
    def _():
        nbits = max(1, (t_len - 1).bit_length())
        cut_ref[...] = _tie_cutoff(
            lambda trial: count(lambda kc, cols: jnp.where(kc == thr_w, jnp.where(cols < wide(trial), 1, 0), 0)),
            need, tq, nbits)

    cut_w = wide(cut_ref[...])

    def write_chunk(c, carry):
        kc = key_ref[c]
        cols = c * ck + lane_cols
        tie = jnp.where(kc == thr_w, jnp.where(cols <= cut_w, 0.0, NEG), NEG)
        sel = jnp.where(kc > thr_w, 0.0, tie)
        bias_ref[c] = jnp.where(cols <= rows, sel, NEG).astype(BF16)
        return carry

    lax.fori_loop(0, nch, write_chunk, 0)

    def fill_chunk(c, carry):
        bias_ref[c] = jnp.full((tq, ck), NEG, BF16)
        return carry

    lax.fori_loop(nch, t_len // ck, fill_chunk, 0)


def dsa_select(q3, wi, k3, *, tq, ck):
    b, _, t_len, _ = q3.shape
    ktop = min(TOPK_MAX, t_len // 4)
    nck = t_len // ck
    return pl.pallas_call(
        functools.partial(_dsa_select_body, tq=tq, ck=ck, t_len=t_len, ktop=ktop),
        grid=(b, t_len // tq),
        in_specs=[
            pl.BlockSpec((None, IDX_HEADS, tq, 2 * LANES), lambda bb, i: (bb, 0, i, 0)),
            pl.BlockSpec((None, tq, IDX_HEADS), lambda bb, i: (bb, i, 0)),
            pl.BlockSpec((None, t_len, 2 * LANES), lambda bb, i: (bb, 0, 0)),
        ],
        out_specs=pl.BlockSpec((None, nck, tq, ck), lambda bb, i: (bb, 0, i, 0)),
        out_shape=jax.ShapeDtypeStruct((b, nck, t_len, ck), BF16),
        scratch_shapes=[pltpu.VMEM((nck, tq, ck), I32), pltpu.VMEM((tq, LANES), I32)],
        compiler_params=_cparams(("arbitrary", "arbitrary")),
        name="dsa_select",
    )(q3, wi, k3)


def _kmax(qi, tq, bk):
    return ((qi + 1) * tq - 1) // bk


def _softmax_update(s, m_ref, l_ref, acc_ref, v, idx, sl):
    m_old = m_ref[idx]
    m_new = jnp.maximum(m_old, jnp.max(s, axis=-1, keepdims=True))
    alpha = jnp.exp(m_old - m_new)
    p = jnp.exp(s - m_new)
    l_ref[idx] = alpha * l_ref[idx] + jnp.sum(p, axis=-1, keepdims=True)
    acc_ref[:, sl] = alpha * acc_ref[:, sl] + _dot(p.astype(BF16), v)
    m_ref[idx] = m_new


def _dsa_attn_body(q_ref, k_ref, v_ref, b_ref, o_ref, m_ref, l_ref, acc_ref, *, tq, bk, nk):
    qi, ki = pl.program_id(1), pl.program_id(2)

    @pl.when(ki == 0)
    def _():
        m_ref[...] = jnp.full_like(m_ref, NEG)
        l_ref[...] = jnp.zeros_like(l_ref)
        acc_ref[...] = jnp.zeros_like(acc_ref)

    @pl.when(ki <= _kmax(qi, tq, bk))
    def _():
        bias = b_ref[...].astype(F32)
        for h in range(N_HEADS):
            sl = slice(h * HEAD_DIM, (h + 1) * HEAD_DIM)
            s = _dot_nt(q_ref[:, sl], k_ref[:, sl]) + bias
            _softmax_update(s, m_ref, l_ref, acc_ref, v_ref[:, sl], h, sl)

    @pl.when(ki == nk - 1)
    def _():
        for h in range(N_HEADS):
            sl = slice(h * HEAD_DIM, (h + 1) * HEAD_DIM)
            o_ref[:, sl] = (acc_ref[:, sl] / l_ref[h]).astype(BF16)


def dsa_attention(q, k, v, bias, *, tq):
    b, t_len, d = q.shape
    bk = bias.shape[3]
    nk = t_len // bk
    kv_spec = pl.BlockSpec((None, bk, d), lambda bb, i, j: (bb, jnp.minimum(j, _kmax(i, tq, bk)), 0))
    return pl.pallas_call(
        functools.partial(_dsa_attn_body, tq=tq, bk=bk, nk=nk),
        grid=(b, t_len // tq, nk),
        in_specs=[
            pl.BlockSpec((None, tq, d), lambda bb, i, j: (bb, i, 0)),
            kv_spec, kv_spec,
            pl.BlockSpec((None, None, tq, bk), lambda bb, i, j: (bb, jnp.minimum(j, _kmax(i, tq, bk)), i, 0)),
        ],
        out_specs=pl.BlockSpec((None, tq, d), lambda bb, i, j: (bb, i, 0)),
        out_shape=jax.ShapeDtypeStruct((b, t_len, d), BF16),
        scratch_shapes=[pltpu.VMEM((N_HEADS, tq, 1), F32), pltpu.VMEM((N_HEADS, tq, 1), F32),
                        pltpu.VMEM((tq, d), F32)],
        compiler_params=_cparams(("arbitrary", "arbitrary", "arbitrary")),
        name="dsa_attention",
    )(q, k, v, bias)


def _diff_lambda(lam_ref, lam_init):
    lp = lam_ref[...]
    a = jnp.sum(lp[0:1] * lp[1:2], axis=-1, keepdims=True)
    b = jnp.sum(lp[2:3] * lp[3:4], axis=-1, keepdims=True)
    return jnp.exp(a) - jnp.exp(b) + lam_init


def _diff_attn_body(q2_ref, k_ref, v_ref, lam_ref, sub_ref, o_ref, m_ref, l_ref, acc_ref,
                    *, tq, bk, nk, lam_init):
    qi, ki = pl.program_id(1), pl.program_id(2)
    kmax = _kmax(qi, tq, bk)

    @pl.when(ki == 0)
    def _():
        m_ref[...] = jnp.full_like(m_ref, NEG)
        l_ref[...] = jnp.zeros_like(l_ref)
        acc_ref[...] = jnp.zeros_like(acc_ref)

    def step(masked):
        q2 = q2_ref[...].reshape(2 * tq, N_HEADS * HEAD_DIM)
        if masked:
            rows = qi * tq + lax.broadcasted_iota(I32, (2 * tq, bk), 0) % tq
            cols = ki * bk + lax.broadcasted_iota(I32, (2 * tq, bk), 1)
            allowed = cols <= rows
        for h in range(N_HEADS):
            sl = slice(h * HEAD_DIM, (h + 1) * HEAD_DIM)
            s = _dot_nt(q2[:, sl], k_ref[:, sl])
            if masked:
                s = jnp.where(allowed, s, NEG)
            _softmax_update(s, m_ref, l_ref, acc_ref, v_ref[:, sl], h, sl)

    @pl.when((ki + 1) * bk - 1 <= qi * tq)
    def _():
        step(False)

    @pl.when(jnp.logical_and((ki + 1) * bk - 1 > qi * tq, ki <= kmax))
    def _():
        step(True)

    @pl.when(ki == nk - 1)
    def _():
        lam = _diff_lambda(lam_ref, lam_init)
        for h in range(N_HEADS):
            sl = slice(h * HEAD_DIM, (h + 1) * HEAD_DIM)
            o = acc_ref[0:tq, sl] / l_ref[h, 0:tq] - lam * (acc_ref[tq:2 * tq, sl] / l_ref[h, tq:2 * tq])
            ms = jnp.mean(o * o, axis=-1, keepdims=True)
            o_ref[:, sl] = (o * lax.rsqrt(ms + NORM_EPS) * sub_ref[...] * (1.0 - lam_init)).astype(BF16)


def diff_attention(q2, k, v, lam_params, subln, lam_init, *, tq, bk):
    b, _, t_len, d = q2.shape
    nk = t_len // bk
    kv_spec = pl.BlockSpec((None, bk, d), lambda bb, i, j: (bb, jnp.minimum(j, _kmax(i, tq, bk)), 0))
    return pl.pallas_call(
        functools.partial(_diff_attn_body, tq=tq, bk=bk, nk=nk, lam_init=lam_init),
        grid=(b, t_len // tq, nk),
        in_specs=[
            pl.BlockSpec((None, 2, tq, d), lambda bb, i, j: (bb, 0, i, 0)),
            kv_spec, kv_spec,
            pl.BlockSpec(lam_params.shape, lambda bb, i, j: (0, 0)),
            pl.BlockSpec((1, HEAD_DIM), lambda bb, i, j: (0, 0)),
        ],
        out_specs=pl.BlockSpec((None, tq, d), lambda bb, i, j: (bb, i, 0)),
        out_shape=jax.ShapeDtypeStruct((b, t_len, d), BF16),
        scratch_shapes=[pltpu.VMEM((N_HEADS, 2 * tq, 1), F32), pltpu.VMEM((N_HEADS, 2 * tq, 1), F32),
                        pltpu.VMEM((2 * tq, d), F32)],
        compiler_params=_cparams(("arbitrary", "arbitrary", "arbitrary")),
        name="diff_attention",
    )(q2, k, v, lam_params, subln)


def _softplus(z):
    return jnp.maximum(z, 0.0) + jnp.log1p(jnp.exp(-jnp.abs(z)))


def _sb_attn_body(q_ref, k_ref, v_ref, u_ref, o_ref, c_ref, acc_ref, *, tq, bk, sub, nk):
    qi, ki = pl.program_id(1), pl.program_id(2)
    kb = _kmax(qi, tq, bk) - ki

    @pl.when(ki == 0)
    def _():
        c_ref[...] = jnp.zeros_like(c_ref)
        acc_ref[...] = jnp.zeros_like(acc_ref)

    @pl.when(kb >= 0)
    def _():
        rows = qi * tq + lax.broadcasted_iota(I32, (tq, sub), 0)
        lane_cols = lax.broadcasted_iota(I32, (tq, sub), 1)
        u = u_ref[...]
        for h in range(N_HEADS):
            sl = slice(h * HEAD_DIM, (h + 1) * HEAD_DIM)
            crun = c_ref[h]
            q = q_ref[:, sl]
            for j in reversed(range(bk // sub)):
                ks = slice(j * sub, (j + 1) * sub)
                z = _dot_nt(q, k_ref[ks, sl])
                mask = kb * bk + j * sub + lane_cols < rows
                sp = _softplus(z)
                lk = jnp.where(mask, -sp, 0.0)
                lk_hi, lk_lo = _split_bf16(lk)
                after = crun + _dot(lk_hi, u) + _dot(lk_lo, u)
                a = jnp.where(mask, jnp.exp(after + z - sp), 0.0)
                acc_ref[:, sl] += _dot(a.astype(BF16), v_ref[ks, sl])
                crun = crun + jnp.sum(lk, axis=-1, keepdims=True)
            c_ref[h] = crun

    @pl.when(ki == nk - 1)
    def _():
        o_ref[...] = acc_ref[...].astype(BF16)


def _later_key_matrix(n):
    j = lax.broadcasted_iota(I32, (n, n), 0)
    s = lax.broadcasted_iota(I32, (n, n), 1)
    return (j > s).astype(BF16)


def sb_attention(q, k, v, *, tq, bk, sub):
    b, t_len, d = q.shape
    nk = t_len // bk
    kv_spec = pl.BlockSpec((None, bk, d), lambda bb, i, j: (bb, jnp.maximum(_kmax(i, tq, bk) - j, 0), 0))
    return pl.pallas_call(
        functools.partial(_sb_attn_body, tq=tq, bk=bk, sub=sub, nk=nk),
        grid=(b, t_len // tq, nk),
        in_specs=[
            pl.BlockSpec((None, tq, d), lambda bb, i, j: (bb, i, 0)),
            kv_spec, kv_spec,
            pl.BlockSpec((sub, sub), lambda bb, i, j: (0, 0)),
        ],
        out_specs=pl.BlockSpec((None, tq, d), lambda bb, i, j: (bb, i, 0)),
        out_shape=jax.ShapeDtypeStruct((b, t_len, d), BF16),
        scratch_shapes=[pltpu.VMEM((N_HEADS, tq, 1), F32), pltpu.VMEM((tq, d), F32)],
        compiler_params=_cparams(("arbitrary", "arbitrary", "arbitrary")),
        name="sb_attention",
    )(q, k, v, _later_key_matrix(sub))


def _sample_scores_body(pt_ref, qh_ref, ql_ref, wi_ref, kn_ref, *rest, n_pages):
    page_refs, out_ref = rest[:n_pages], rest[n_pages]
    qh, ql = qh_ref[...], ql_ref[...]
    w = wi_ref[...]
    for p in range(n_pages):
        k_hi, k_lo = _split_bf16(page_refs[p][...])
        s = _dot_nt(qh, k_hi) + _dot_nt(ql, k_hi) + _dot_nt(qh, k_lo)
        out_ref[p:p + 1, :] = jnp.sum(jnp.maximum(s, 0.0) * w, axis=0, keepdims=True)
    q = qh.astype(F32) + ql.astype(F32)
    s_new = jnp.sum(q * kn_ref[...], axis=-1, keepdims=True)
    i_new = jnp.sum(jnp.maximum(s_new, 0.0) * w, axis=0, keepdims=True)
    lane = lax.broadcasted_iota(I32, (1, LANES), 1)
    out_ref[n_pages:n_pages + 1, :] = jnp.where(lane == 0, i_new, -jnp.inf)


def sample_scores(page_table, qh, ql, wi_col, ki_new, pool_kidx, layer):
    db, n_pages = page_table.shape
    page_specs = [
        pl.BlockSpec((None, None, PAGE_SIZE, IDX_DIM),
                     functools.partial(lambda r, pt, p: (layer, pt[r, p], 0, 0), p=p))
        for p in range(n_pages)]
    grid_spec = pltpu.PrefetchScalarGridSpec(
        num_scalar_prefetch=1,
        grid=(db,),
        in_specs=[
            pl.BlockSpec((None, IDX_HEADS, IDX_DIM), lambda r, pt: (r, 0, 0)),
            pl.BlockSpec((None, IDX_HEADS, IDX_DIM), lambda r, pt: (r, 0, 0)),
            pl.BlockSpec((None, IDX_HEADS, 1), lambda r, pt: (r, 0, 0)),
            pl.BlockSpec((None, 1, IDX_DIM), lambda r, pt: (r, 0, 0)),
        ] + page_specs,
        out_specs=pl.BlockSpec((None, n_pages + 1, LANES), lambda r, pt: (r, 0, 0)),
    )
    return pl.pallas_call(
        functools.partial(_sample_scores_body, n_pages=n_pages),
        grid_spec=grid_spec,
        out_shape=jax.ShapeDtypeStruct((db, n_pages + 1, LANES), F32),
        compiler_params=_cparams(("arbitrary",)),
        name="sample_scores",
    )(page_table, qh, ql, wi_col, ki_new, *([pool_kidx] * n_pages))


def _sample_select_body(sc_ref, e_ref, bias_ref, *, db, n_chunks, ktop):
    lane = lax.broadcasted_iota(I32, (db, LANES), 1)
    keys = []
    for c in range(n_chunks):
        s = sc_ref[:, c * LANES:(c + 1) * LANES]
        keys.append(jnp.where(s == -jnp.inf, INT_MIN, _sort_key(s)))

    def count(ind):
        cnt = jnp.zeros((db, LANES), I32)
        for c in range(n_chunks):
            cnt = cnt + ind(keys[c], c * LANES + lane)
        return _lane_total(cnt)

    thr = _kth_largest(lambda trial: count(lambda kc, cols: jnp.where(kc >= trial, 1, 0)), db, ktop)
    need = ktop - count(lambda kc, cols: jnp.where(kc > thr, 1, 0))
    nbits = max(1, (n_chunks * LANES - 1).bit_length())
    cut = _tie_cutoff(
        lambda trial: count(lambda kc, cols: jnp.where(kc == thr, jnp.where(cols < trial, 1, 0), 0)),
        need, db, nbits)
    e = e_ref[...]
    for c in range(n_chunks):
        cols = c * LANES + lane
        tie = jnp.where(keys[c] == thr, jnp.where(cols <= cut, 1.0, 0.0), 0.0)
        sel = jnp.where(keys[c] > thr, 1.0, tie)
        sel = jnp.where(keys[c] == INT_MIN, 0.0, sel)
        wide = _dot(sel.astype(BF16), e)
        bias_ref[c] = jnp.where(wide > 0.5, 0.0, NEG)


def sample_select(scores2d, ktop):
    db, n = scores2d.shape
    n_chunks = n // LANES
    flat = N_HEADS * PAGE_SIZE
    t_of_lane = lax.broadcasted_iota(I32, (PAGE_SIZE, flat), 1) // N_HEADS
    e = (t_of_lane == lax.broadcasted_iota(I32, (PAGE_SIZE, flat), 0)).astype(BF16)
    return pl.pallas_call(
        functools.partial(_sample_select_body, db=db, n_chunks=n_chunks, ktop=ktop),
        grid=(1,),
        in_specs=[pl.BlockSpec((db, n), lambda i: (0, 0)), pl.BlockSpec((PAGE_SIZE, flat), lambda i: (0, 0))],
        out_specs=pl.BlockSpec((n_chunks, db, flat), lambda i: (0, 0, 0)),
        out_shape=jax.ShapeDtypeStruct((n_chunks, db, flat), F32),
        compiler_params=_cparams(("arbitrary",)),
        name="sample_select",
    )(scores2d, e)


def _suffix_sum_stride8(x):
    n = x.shape[1]
    lane = lax.broadcasted_iota(I32, x.shape, 1)
    sh = N_HEADS
    while sh < n:
        x = x + jnp.where(lane < n - sh, pltpu.roll(x, n - sh, 1), 0.0)
        sh *= 2
    return x


def _sample_attn_body(pt_ref, q_ref, kn_ref, vn_ref, *rest, mode, pps, n_steps, lam_init):
    k_refs, v_refs = rest[:pps], rest[pps:2 * pps]
    rest = rest[2 * pps:]
    if mode == "dsa":
        bias_ref, bnew_ref, o_ref, m_ref, l_ref, acc_ref = rest
    elif mode == "diff":
        lam_ref, sub_ref, o_ref, m_ref, l_ref, acc_ref = rest
    else:
        o_ref, m_ref, l_ref, acc_ref = rest
    s_id = pl.program_id(1)
    nrow = q_ref.shape[0]
    flat = N_HEADS * PAGE_SIZE
    q = q_ref[...]
    valid = (lax.broadcasted_iota(I32, (nrow, flat), 1) % N_HEADS
             == lax.broadcasted_iota(I32, (nrow, flat), 0) % N_HEADS)

    @pl.when(s_id == 0)
    def _():
        m_ref[...] = jnp.full_like(m_ref, 0.0 if mode == "sb" else NEG)
        l_ref[...] = jnp.zeros_like(l_ref)
        acc_ref[...] = jnp.zeros_like(acc_ref)

    order = reversed(range(pps)) if mode == "sb" else range(pps)
    for i in order:
        k2d = k_refs[i][...].reshape(flat, HEAD_DIM).astype(BF16)
        v2d = v_refs[i][...].reshape(flat, HEAD_DIM).astype(BF16)
        s_all = _dot_nt(q, k2d)
        if mode == "sb":
            sp = _softplus(s_all)
            lk = jnp.where(valid, -sp, 0.0)
            incl = _suffix_sum_stride8(lk)
            after = m_ref[...] + (incl - lk)
            a = jnp.where(valid, jnp.exp(after + s_all - sp), 0.0)
            acc_ref[...] += _dot(a.astype(BF16), v2d)
            m_ref[...] += jnp.sum(lk, axis=-1, keepdims=True)
        else:
            s = jnp.where(valid, s_all, NEG)
            if mode == "dsa":
                s = s + bias_ref[i:i + 1, :]
            m_old = m_ref[...]
            m_new = jnp.maximum(m_old, jnp.max(s, axis=-1, keepdims=True))
            alpha = jnp.exp(m_old - m_new)
            p = jnp.where(valid, jnp.exp(s - m_new), 0.0)
            l_ref[...] = alpha * l_ref[...] + jnp.sum(p, axis=-1, keepdims=True)
            acc_ref[...] = alpha * acc_ref[...] + _dot(p.astype(BF16), v2d)
            m_ref[...] = m_new

    @pl.when(s_id == n_steps - 1)
    def _():
        reps = nrow // N_HEADS
        k_new = jnp.concatenate([kn_ref[...]] * reps, axis=0)
        v_new = jnp.concatenate([vn_ref[...]] * reps, axis=0)
        if mode == "sb":
            o_ref[...] = acc_ref[...] + jnp.zeros((nrow, 1), F32) * v_new
            return
        s_new = jnp.sum(q.astype(F32) * k_new, axis=-1, keepdims=True)
        if mode == "dsa":
            s_new = s_new + bnew_ref[:, 0:1]
        m_old = m_ref[...]
        m_new = jnp.maximum(m_old, s_new)
        alpha = jnp.exp(m_old - m_new)
        p_new = jnp.exp(s_new - m_new)
        l_fin = alpha * l_ref[...] + p_new
        o = (alpha * acc_ref[...] + p_new * v_new) / l_fin
        if mode == "diff":
            lam = _diff_lambda(lam_ref, lam_init)
            o = o[0:N_HEADS] - lam * o[N_HEADS:2 * N_HEADS]
            ms = jnp.mean(o * o, axis=-1, keepdims=True)
            o = o * lax.rsqrt(ms + NORM_EPS) * sub_ref[...] * (1.0 - lam_init)
        o_ref[...] = o


def sample_attention(mode, page_table, q_rows, k_new, v_new, pool_k, pool_v, layer, *, pps,
                     bias_pages=None, bias_new=None, lam_params=None, subln=None, lam_init=0.0):
    db, n_pages = page_table.shape
    nrow = q_rows.shape[1]
    n_steps = n_pages // pps
    flat = N_HEADS * PAGE_SIZE

    def page_map(i):
        if mode == "sb":
            return lambda r, s, pt: (layer, pt[r, (n_steps - 1 - s) * pps + i], 0, 0, 0)
        return lambda r, s, pt: (layer, pt[r, s * pps + i], 0, 0, 0)

    page_specs = [pl.BlockSpec((None, None, PAGE_SIZE, N_HEADS, HEAD_DIM), page_map(i)) for i in range(pps)]
    in_specs = [
        pl.BlockSpec((None, nrow, HEAD_DIM), lambda r, s, pt: (r, 0, 0)),
        pl.BlockSpec((None, N_HEADS, HEAD_DIM), lambda r, s, pt: (r, 0, 0)),
        pl.BlockSpec((None, N_HEADS, HEAD_DIM), lambda r, s, pt: (r, 0, 0)),
    ] + page_specs + page_specs
    args = [q_rows, k_new, v_new] + [pool_k] * pps + [pool_v] * pps
    if mode == "dsa":
        in_specs += [pl.BlockSpec((None, None, pps, flat), lambda r, s, pt: (r, s, 0, 0)),
                     pl.BlockSpec((None, 1, flat), lambda r, s, pt: (r, 0, 0))]
        args += [bias_pages, bias_new]
    elif mode == "diff":
        in_specs += [pl.BlockSpec(lam_params.shape, lambda r, s, pt: (0, 0)),
                     pl.BlockSpec((1, HEAD_DIM), lambda r, s, pt: (0, 0))]
        args += [lam_params, subln]
    grid_spec = pltpu.PrefetchScalarGridSpec(
        num_scalar_prefetch=1,
        grid=(db, n_steps),
        in_specs=in_specs,
        out_specs=pl.BlockSpec((None, N_HEADS, HEAD_DIM), lambda r, s, pt: (r, 0, 0)),
        scratch_shapes=[pltpu.VMEM((nrow, 1), F32), pltpu.VMEM((nrow, 1), F32),
                        pltpu.VMEM((nrow, HEAD_DIM), F32)],
    )
    return pl.pallas_call(
        functools.partial(_sample_attn_body, mode=mode, pps=pps, n_steps=n_steps, lam_init=lam_init),
        grid_spec=grid_spec,
        out_shape=jax.ShapeDtypeStruct((db, N_HEADS, HEAD_DIM), F32),
        compiler_params=_cparams(("arbitrary", "arbitrary")),
        name="sample_attention_" + mode,
    )(page_table, *args)


def _idx_weight(w_in):
    d = w_in.shape[0]
    hd = N_HEADS * HEAD_DIM
    base = 3 * hd
    cols = []
    zeros64 = jnp.zeros((d, LANES - IDX_DIM), w_in.dtype)
    for h in range(IDX_HEADS):
        cols += [w_in[:, base + h * IDX_DIM:base + (h + 1) * IDX_DIM], zeros64]
    cols += [w_in[:, base + IDX_HEADS * IDX_DIM:base + IDX_HEADS * IDX_DIM + IDX_DIM], zeros64]
    wi = w_in[:, base + IDX_HEADS * IDX_DIM + IDX_DIM:]
    cols += [wi, jnp.zeros((d, LANES - wi.shape[1]), w_in.dtype)]
    return jnp.concatenate(cols, axis=1)


def kernel(x_prompt, x_sample, cache_a_k, cache_a_v, cache_a_kidx, cache_b_k, cache_b_v, cache_c_k, cache_c_v, page_table, c_prompt, c_sample, ada_w, ada_b, norm_mix, norm_ffn, ffn_w1, ffn_w2, a_w_in, a_q_norm, a_k_norm, a_idx_k_norm, a_w_out, b_w_in, b_q_norm, b_k_norm, b_lambda, b_subln, b_w_out, c_w_in, c_w_out):
    b, t_len, d = x_prompt.shape
    db = x_sample.shape[0]
    depth = ada_w.shape[0]
    n_pages = page_table.shape[1]
    past = n_pages * PAGE_SIZE
    hd = (N_HEADS, HEAD_DIM)

    n_c = b + db
    n_c_pad = -(-n_c // 8) * 8
    c_all = jnp.concatenate([c_prompt, c_sample, jnp.zeros((n_c_pad - n_c, d), F32)], axis=0)
    mod = ada_modulation_all(c_all, ada_w, ada_b)

    tabs128_p = rope_tables(0, t_len, 64)
    tabs64_p = rope_tables(0, t_len, 32)
    tabs128_s = tuple(x[0:1] for x in rope_tables(past, 8, 64))
    tabs64_s = tuple(x[0:1] for x in rope_tables(past, 8, 32))

    xp = x_prompt
    xs = x_sample.reshape(1, db, d)
    rows = {name: [] for name in ("a_k", "a_v", "a_kidx", "b_k", "b_v", "c_k", "c_v")}
    vec = lambda a: a.reshape(1, -1)

    for i in range(depth):
        kind, j = i % N_MIXERS, i // N_MIXERS
        mp = [mod[i, :b, k * d:(k + 1) * d].reshape(b, 1, d) for k in range(6)]
        ms = [mod[i, b:n_c, k * d:(k + 1) * d].reshape(1, db, d) for k in range(6)]
        g_mix, g_ffn = vec(norm_mix[i]), vec(norm_ffn[i])

        if kind == 0:
            w_in = a_w_in[j]
            w_qkv = (w_in[:, :3 * d].astype(BF16),)
            w_idx = _split_bf16(_idx_weight(w_in))
            qn, kn = vec(a_q_norm[j]), vec(a_k_norm[j])
            ikn = jnp.pad(a_idx_k_norm[j], (0, LANES - IDX_DIM)).reshape(1, LANES)
            w_out = a_w_out[j].astype(BF16)

            raw = proj(xp, g_mix, mp[0], mp[1], w_qkv, tm=512, tn=1024, name="proj_dsa")
            idx = proj(xp, g_mix, mp[0], mp[1], w_idx, tm=512, tn=w_idx[0].shape[1], name="proj_dsa_idx")
            q_bf, k_f, k_bf, v_bf, q3, ki_f, k3, wi = post_a(raw, idx, tabs128_p, tabs64_p, qn, kn, ikn, tm=256)
            bias = dsa_select(q3, wi, k3, tq=128, ck=512)
            op = dsa_attention(q_bf, k_bf, v_bf, bias, tq=512)
            kp, vp, kip = k_f.reshape(b, t_len, *hd), raw[:, :, 2 * d:].reshape(b, t_len, *hd), ki_f

            raw_s = proj(xs, g_mix, ms[0], ms[1], w_qkv, tm=128, tn=1024, name="proj_dsa")
            idx_s = proj(xs, g_mix, ms[0], ms[1], w_idx, tm=128, tn=w_idx[0].shape[1], name="proj_dsa_idx")
            qs_bf, ks_f, _, _, q3_s, kis_f, _, wi_s = post_a(raw_s, idx_s, tabs128_s, tabs64_s, qn, kn, ikn, tm=128)
            qh = jnp.transpose(q3_s[0, :, :, 0:IDX_DIM], (1, 0, 2))
            ql = jnp.transpose(q3_s[0, :, :, IDX_DIM:2 * IDX_DIM], (1, 0, 2))
            scores = sample_scores(page_table, qh, ql, wi_s.reshape(db, IDX_HEADS, 1),
                                   kis_f.reshape(db, 1, IDX_DIM), cache_a_kidx, j)
            ktop = min(TOPK_MAX, (past + 1) // 4)
            sel_bias = sample_select(scores.reshape(db, (n_pages + 1) * LANES), ktop)
            pps = 8
            bias_pages = jnp.transpose(sel_bias[:n_pages], (1, 0, 2)).reshape(db, n_pages // pps, pps, -1)
            bias_new = jnp.transpose(sel_bias[n_pages:], (1, 0, 2))
            ks_, vs_ = ks_f.reshape(db, *hd), raw_s[0, :, 2 * d:].reshape(db, *hd)
            os_ = sample_attention("dsa", page_table, qs_bf.reshape(db, *hd), ks_, vs_, cache_a_k, cache_a_v, j,
                                   pps=pps, bias_pages=bias_pages, bias_new=bias_new)
            rows["a_k"].append((kp, ks_))
            rows["a_v"].append((vp, vs_))
            rows["a_kidx"].append((kip, kis_f.reshape(db, 1, IDX_DIM)))
        elif kind == 1:
            lam_init = 0.8 - 0.6 * math.exp(-0.3 * i)
            w_in = (b_w_in[j].astype(BF16),)
            qn = vec(jnp.tile(b_q_norm[j], 2))
            kn = vec(jnp.tile(b_k_norm[j], 2))
            subln = vec(b_subln[j])
            w_out = b_w_out[j].astype(BF16)

            raw = proj(xp, g_mix, mp[0], mp[1], w_in, tm=512, tn=1024, name="proj_diff")
            q2, k_f, k_bf, v_bf = post_b(raw, tabs64_p, qn, kn, tm=256)
            op = diff_attention(q2, k_bf, v_bf, b_lambda[j], subln, lam_init, tq=512, bk=512)
            kp, vp = k_f.reshape(b, t_len, *hd), raw[:, :, 2 * d:].reshape(b, t_len, *hd)

            raw_s = proj(xs, g_mix, ms[0], ms[1], w_in, tm=128, tn=1024, name="proj_diff")
            q2_s, ks_f, _, _ = post_b(raw_s, tabs64_s, qn, kn, tm=128)
            q_rows = jnp.transpose(q2_s[0].reshape(2, db, *hd), (1, 0, 2, 3)).reshape(db, 2 * N_HEADS, HEAD_DIM)
            ks_, vs_ = ks_f.reshape(db, *hd), raw_s[0, :, 2 * d:].reshape(db, *hd)
            os_ = sample_attention("diff", page_table, q_rows, ks_, vs_, cache_b_k, cache_b_v, j, pps=8,
                                   lam_params=b_lambda[j], subln=subln, lam_init=lam_init)
            rows["b_k"].append((kp, ks_))
            rows["b_v"].append((vp, vs_))
        else:
            w_in = (c_w_in[j].astype(BF16),)
            w_out = c_w_out[j].astype(BF16)

            raw = proj(xp, g_mix, mp[0], mp[1], w_in, tm=512, tn=1024, name="proj_sb")
            q_bf, k_bf, v_bf = post_c(raw, tm=256)
            op = sb_attention(q_bf, k_bf, v_bf, tq=512, bk=512, sub=256)
            kp, vp = raw[:, :, d:2 * d].reshape(b, t_len, *hd), raw[:, :, 2 * d:].reshape(b, t_len, *hd)

            raw_s = proj(xs, g_mix, ms[0], ms[1], w_in, tm=128, tn=1024, name="proj_sb")
            qs_bf, _, _ = post_c(raw_s, tm=128)
            ks_, vs_ = raw_s[0, :, d:2 * d].reshape(db, *hd), raw_s[0, :, 2 * d:].reshape(db, *hd)
            os_ = sample_attention("sb", page_table, qs_bf.reshape(db, *hd), ks_, vs_, cache_c_k, cache_c_v, j, pps=8)
            rows["c_k"].append((kp, ks_))
            rows["c_v"].append((vp, vs_))

        w1, w2 = ffn_w1[i].astype(BF16), ffn_w2[i].astype(BF16)
        xp = out_proj_residual(op, w_out, xp, mp[2], tm=512)
        xp = mlp_residual(xp, g_ffn, mp[3], mp[4], mp[5], w1, w2, tm=1024, tf=1024)
        xs = out_proj_residual(os_.reshape(1, db, d).astype(BF16), w_out, xs, ms[2], tm=128)
        xs = mlp_residual(xs, g_ffn, ms[3], ms[4], ms[5], w1, w2, tm=128, tf=1024)

    def stack(name, which):
        return jnp.stack([r[which] for r in rows[name]])

    outs = [xp, xs.reshape(db, 1, d)]
    for which in (0, 1):
        for name in ("a_k", "a_v", "a_kidx", "b_k", "b_v", "c_k", "c_v"):
            y = stack(name, which)
            if which == 1 and name != "a_kidx":
                y = y.reshape(y.shape[0], db, 1, *hd)
            outs.append(y)
    return tuple(outs)
```

```python
import functools
import math

import jax
import jax.numpy as jnp
from jax import lax
from jax.experimental import pallas as pl
from jax.experimental.pallas import tpu as pltpu

F32 = jnp.float32
BF16 = jnp.bfloat16
I32 = jnp.int32
I16 = jnp.int16

D_MODEL = 1024
N_HEADS = 8
HEAD_DIM = 128
IDX_HEADS = 8
IDX_DIM = 64
TOPK_MAX = 256
PAGE_SIZE = 128
ROPE_THETA = 10000.0
NORM_EPS = 1e-6
N_MIXERS = 3

LANES = 128
NEG = -1e30
SB_EXP_UNDERFLOW = -110.0
INT_MIN = -2147483648
VMEM_LIMIT = 56 * 1024 * 1024
NT_DIMS = (((1,), (1,)), ((), ()))


def _cparams(sem):
    return pltpu.CompilerParams(dimension_semantics=sem, vmem_limit_bytes=VMEM_LIMIT)


def _split_bf16(x):
    hi = x.astype(BF16)
    lo = (x - hi.astype(F32)).astype(BF16)
    return hi, lo


def _dot(a, b):
    return jnp.dot(a, b, preferred_element_type=F32)


def _dot_nt(a, b):
    return lax.dot_general(a, b, NT_DIMS, preferred_element_type=F32)


def _norm_mod(x, g, sc, sh):
    ms = jnp.mean(x * x, axis=-1, keepdims=True)
    return (x * lax.rsqrt(ms + NORM_EPS) * g) * (1.0 + sc) + sh


def _mod_spec(rm, tm, d):
    if rm == 1:
        return pl.BlockSpec((None, 1, d), lambda g, i, j: (g, 0, 0))
    return pl.BlockSpec((None, tm, d), lambda g, i, j: (g, i, 0))


def _ada_body(c_ref, w_ref, b_ref, o_ref):
    c = c_ref[...]
    s = c * jax.nn.sigmoid(c)
    s_hi, s_lo = _split_bf16(s)
    w_hi, w_lo = _split_bf16(w_ref[...])
    o_ref[...] = _dot(s_hi, w_hi) + _dot(s_lo, w_hi) + _dot(s_hi, w_lo) + b_ref[...]


def ada_modulation_all(c_all, ada_w, ada_b):
    depth, d, n = ada_w.shape
    r = c_all.shape[0]
    tn = 1536
    return pl.pallas_call(
        _ada_body,
        grid=(depth, n // tn),
        in_specs=[
            pl.BlockSpec((r, d), lambda i, j: (0, 0)),
            pl.BlockSpec((None, d, tn), lambda i, j: (i, 0, j)),
            pl.BlockSpec((None, 1, tn), lambda i, j: (i, 0, j)),
        ],
        out_specs=pl.BlockSpec((None, r, tn), lambda i, j: (i, 0, j)),
        out_shape=jax.ShapeDtypeStruct((depth, r, n), F32),
        compiler_params=_cparams(("arbitrary", "arbitrary")),
        name="ada_modulation",
    )(c_all, ada_w, ada_b.reshape(depth, 1, n))


def _rope_tab_body(invf_ref, sgn_ref, cos_ref, sin_ref, *, start, tr):
    i = pl.program_id(0)
    pos = (start + i * tr + lax.broadcasted_iota(I32, (tr, LANES), 0)).astype(F32)
    ang = pos * invf_ref[...]
    cos_ref[...] = jnp.cos(ang)
    sin_ref[...] = jnp.sin(ang) * sgn_ref[...]


def rope_tables(start, rows, half):
    inv_freq = ROPE_THETA ** (-jnp.arange(half, dtype=F32) / half)
    reps = LANES // (2 * half)
    invf = jnp.tile(jnp.concatenate([inv_freq, inv_freq]), reps).reshape(1, LANES)
    sgn = jnp.tile(jnp.concatenate([-jnp.ones((half,), F32), jnp.ones((half,), F32)]), reps).reshape(1, LANES)
    tr = min(rows, 512)
    return pl.pallas_call(
        functools.partial(_rope_tab_body, start=start, tr=tr),
        grid=(rows // tr,),
        in_specs=[pl.BlockSpec((1, LANES), lambda i: (0, 0)), pl.BlockSpec((1, LANES), lambda i: (0, 0))],
        out_specs=[pl.BlockSpec((tr, LANES), lambda i: (i, 0)), pl.BlockSpec((tr, LANES), lambda i: (i, 0))],
        out_shape=[jax.ShapeDtypeStruct((rows, LANES), F32)] * 2,
        compiler_params=_cparams(("arbitrary",)),
        name="rope_tables",
    )(invf, sgn)


def _proj_body(x_ref, sh_ref, sc_ref, g_ref, w_ref, o_ref, hn_ref):
    @pl.when(pl.program_id(2) == 0)
    def _():
        hn_ref[...] = _norm_mod(x_ref[...], g_ref[...], sc_ref[...], sh_ref[...]).astype(BF16)

    o_ref[...] = _dot(hn_ref[...], w_ref[...])


def _proj3_body(x_ref, sh_ref, sc_ref, g_ref, whi_ref, wlo_ref, o_ref, hhi_ref, hlo_ref):
    @pl.when(pl.program_id(2) == 0)
    def _():
        hi, lo = _split_bf16(_norm_mod(x_ref[...], g_ref[...], sc_ref[...], sh_ref[...]))
        hhi_ref[...] = hi
        hlo_ref[...] = lo

    o_ref[...] = (_dot(hhi_ref[...], whi_ref[...]) + _dot(hlo_ref[...], whi_ref[...])
                  + _dot(hhi_ref[...], wlo_ref[...]))


def proj(x, g, sh, sc, ws, *, tm, tn, name):
    grp, r, d = x.shape
    n = ws[0].shape[1]
    tm = min(tm, r)
    precise = len(ws) == 2
    w_spec = pl.BlockSpec((d, tn), lambda b, i, j: (0, j))
    return pl.pallas_call(
        _proj3_body if precise else _proj_body,
        grid=(grp, r // tm, n // tn),
        in_specs=[
            pl.BlockSpec((None, tm, d), lambda b, i, j: (b, i, 0)),
            _mod_spec(sh.shape[1], tm, d),
            _mod_spec(sc.shape[1], tm, d),
            pl.BlockSpec((1, d), lambda b, i, j: (0, 0)),
        ] + [w_spec] * len(ws),
        out_specs=pl.BlockSpec((None, tm, tn), lambda b, i, j: (b, i, j)),
        out_shape=jax.ShapeDtypeStruct((grp, r, n), F32),
        scratch_shapes=[pltpu.VMEM((tm, d), BF16)] * len(ws),
        compiler_params=_cparams(("arbitrary", "arbitrary", "arbitrary")),
        name=name,
    )(x, sh, sc, g, *ws)


def _oproj_body(o_ref, w_ref, x_ref, gt_ref, y_ref):
    y_ref[...] = x_ref[...] + gt_ref[...] * _dot(o_ref[...], w_ref[...])


def out_proj_residual(o, w, x, gate, *, tm):
    grp, r, d = x.shape
    tm = min(tm, r)
    return pl.pallas_call(
        _oproj_body,
        grid=(grp, r // tm, 1),
        in_specs=[
            pl.BlockSpec((None, tm, d), lambda b, i, j: (b, i, 0)),
            pl.BlockSpec((d, d), lambda b, i, j: (0, 0)),
            pl.BlockSpec((None, tm, d), lambda b, i, j: (b, i, 0)),
            _mod_spec(gate.shape[1], tm, d),
        ],
        out_specs=pl.BlockSpec((None, tm, d), lambda b, i, j: (b, i, 0)),
        out_shape=jax.ShapeDtypeStruct((grp, r, d), F32),
        compiler_params=_cparams(("arbitrary", "arbitrary", "arbitrary")),
        name="out_proj_residual",
    )(o, w, x, gate)


def _mlp_body(x_ref, sh_ref, sc_ref, gt_ref, g_ref, w1_ref, w2_ref, y_ref, hn_ref, acc_ref, *, nf):
    f = pl.program_id(2)

    @pl.when(f == 0)
    def _():
        hn_ref[...] = _norm_mod(x_ref[...], g_ref[...], sc_ref[...], sh_ref[...]).astype(BF16)
        acc_ref[...] = jnp.zeros_like(acc_ref)

    h = jnp.maximum(_dot(hn_ref[...], w1_ref[...]), 0.0)
    acc_ref[...] += _dot((h * h).astype(BF16), w2_ref[...])

    @pl.when(f == nf - 1)
    def _():
        y_ref[...] = x_ref[...] + gt_ref[...] * acc_ref[...]


def mlp_residual(x, g, sh, sc, gate, w1, w2, *, tm, tf):
    grp, r, d = x.shape
    dff = w1.shape[1]
    tm = min(tm, r)
    nf = dff // tf
    return pl.pallas_call(
        functools.partial(_mlp_body, nf=nf),
        grid=(grp, r // tm, nf),
        in_specs=[
            pl.BlockSpec((None, tm, d), lambda b, i, f: (b, i, 0)),
            _mod_spec(sh.shape[1], tm, d),
            _mod_spec(sc.shape[1], tm, d),
            _mod_spec(gate.shape[1], tm, d),
            pl.BlockSpec((1, d), lambda b, i, f: (0, 0)),
            pl.BlockSpec((d, tf), lambda b, i, f: (0, f)),
            pl.BlockSpec((tf, d), lambda b, i, f: (f, 0)),
        ],
        out_specs=pl.BlockSpec((None, tm, d), lambda b, i, f: (b, i, 0)),
        out_shape=jax.ShapeDtypeStruct((grp, r, d), F32),
        scratch_shapes=[pltpu.VMEM((tm, d), BF16), pltpu.VMEM((tm, d), F32)],
        compiler_params=_cparams(("arbitrary", "arbitrary", "arbitrary")),
        name="mlp_residual",
    )(x, sh, sc, gate, g, w1, w2)


def _rms_lanes(x, gain, width):
    sq = x * x
    if width == LANES:
        ms = jnp.mean(sq, axis=-1, keepdims=True)
    else:
        low = lax.broadcasted_iota(I32, (1, LANES), 1) < width
        s_lo = jnp.sum(jnp.where(low, sq, 0.0), axis=-1, keepdims=True)
        s_hi = jnp.sum(jnp.where(low, 0.0, sq), axis=-1, keepdims=True)
        ms = jnp.where(low, s_lo, s_hi) * (1.0 / width)
    return x * lax.rsqrt(ms + NORM_EPS) * gain


def _rope128(x, cos, sin_s):
    return x * cos + pltpu.roll(x, 64, 1) * sin_s


def _rope64(x, cos, sin_s):
    first = (lax.broadcasted_iota(I32, (1, LANES), 1) % 64) < 32
    partner = jnp.where(first, pltpu.roll(x, 96, 1), pltpu.roll(x, 32, 1))
    return x * cos + partner * sin_s


def _post_a_body(raw_ref, idx_ref, c128_ref, s128_ref, c64_ref, s64_ref, qn_ref, kn_ref, ikn_ref,
                 q_ref, kf_ref, kb_ref, vt_ref, q3_ref, kif_ref, k3_ref, wi_ref):
    c128, s128 = c128_ref[...], s128_ref[...]
    c64, s64 = c64_ref[...], s64_ref[...]
    d = N_HEADS * HEAD_DIM
    for h in range(N_HEADS):
        sl = slice(h * HEAD_DIM, (h + 1) * HEAD_DIM)
        q = _rope128(_rms_lanes(raw_ref[:, sl], qn_ref[...], LANES), c128, s128)
        q_ref[:, sl] = (q * HEAD_DIM ** -0.5).astype(BF16)
        k = _rope128(_rms_lanes(raw_ref[:, d + h * HEAD_DIM:d + (h + 1) * HEAD_DIM], kn_ref[...], LANES), c128, s128)
        kf_ref[:, sl] = k
        kb_ref[:, sl] = k.astype(BF16)
        qi = _rope64(idx_ref[:, sl], c64, s64) * IDX_DIM ** -0.5
        qi_hi, qi_lo = _split_bf16(qi)
        q3_ref[h, :, 0:LANES] = (qi_hi.astype(F32) + pltpu.roll(qi_lo.astype(F32), 64, 1)).astype(BF16)
        q3_ref[h, :, LANES:2 * LANES] = qi_hi
        vt_ref[h] = raw_ref[:, 2 * d + h * HEAD_DIM:2 * d + (h + 1) * HEAD_DIM].T.astype(BF16)
    ki_raw = idx_ref[:, d:d + LANES]
    ms = jnp.sum(ki_raw * ki_raw, axis=-1, keepdims=True) * (1.0 / IDX_DIM)
    ki = _rope64(ki_raw * lax.rsqrt(ms + NORM_EPS) * ikn_ref[...], c64, s64)
    kif_ref[...] = ki[:, :IDX_DIM]
    ki_hi, ki_lo = _split_bf16(ki)
    k3_ref[:, 0:LANES] = (ki_hi.astype(F32) + pltpu.roll(ki_hi.astype(F32), 64, 1)).astype(BF16)
    k3_ref[:, LANES:2 * LANES] = ki_lo
    wi_ref[...] = idx_ref[:, d + LANES:d + LANES + IDX_HEADS] * IDX_HEADS ** -0.5


def post_a(raw, idx, tabs128, tabs64, qn, kn, ikn, *, tm):
    grp, r, _ = raw.shape
    tm = min(tm, r)
    d = N_HEADS * HEAD_DIM
    rt = tabs128[0].shape[0]
    tab_spec = (pl.BlockSpec((1, LANES), lambda b, i: (0, 0)) if rt == 1
                else pl.BlockSpec((tm, LANES), lambda b, i: (i, 0)))
    vec_spec = pl.BlockSpec((1, LANES), lambda b, i: (0, 0))
    row = lambda w: pl.BlockSpec((None, tm, w), lambda b, i: (b, i, 0))
    return pl.pallas_call(
        _post_a_body,
        grid=(grp, r // tm),
        in_specs=[row(3 * d), row(idx.shape[2]), tab_spec, tab_spec, tab_spec, tab_spec,
                  vec_spec, vec_spec, vec_spec],
        out_specs=[row(d), row(d), row(d),
                   pl.BlockSpec((None, N_HEADS, HEAD_DIM, tm), lambda b, i: (b, 0, 0, i)),
                   pl.BlockSpec((None, IDX_HEADS, tm, 2 * LANES), lambda b, i: (b, 0, i, 0)),
                   row(IDX_DIM), row(2 * LANES), row(IDX_HEADS)],
        out_shape=[jax.ShapeDtypeStruct((grp, r, d), BF16), jax.ShapeDtypeStruct((grp, r, d), F32),
                   jax.ShapeDtypeStruct((grp, r, d), BF16), jax.ShapeDtypeStruct((grp, N_HEADS, HEAD_DIM, r), BF16),
                   jax.ShapeDtypeStruct((grp, IDX_HEADS, r, 2 * LANES), BF16),
                   jax.ShapeDtypeStruct((grp, r, IDX_DIM), F32),
                   jax.ShapeDtypeStruct((grp, r, 2 * LANES), BF16),
                   jax.ShapeDtypeStruct((grp, r, IDX_HEADS), F32)],
        compiler_params=_cparams(("arbitrary", "arbitrary")),
        name="post_dsa",
    )(raw, idx, tabs128[0], tabs128[1], tabs64[0], tabs64[1], qn, kn, ikn)


def _post_b_body(raw_ref, c64_ref, s64_ref, qn_ref, kn_ref, q2_ref, kf_ref, kb_ref, vt_ref):
    c64, s64 = c64_ref[...], s64_ref[...]
    d = N_HEADS * HEAD_DIM
    low = lax.broadcasted_iota(I32, (1, LANES), 1) < 64
    for h in range(N_HEADS):
        sl = slice(h * HEAD_DIM, (h + 1) * HEAD_DIM)
        q = _rope64(_rms_lanes(raw_ref[:, sl], qn_ref[...], 64), c64, s64) * (HEAD_DIM // 2) ** -0.5
        q2_ref[0, :, sl] = jnp.where(low, q, 0.0).astype(BF16)
        q2_ref[1, :, sl] = jnp.where(low, 0.0, q).astype(BF16)
        k = _rope64(_rms_lanes(raw_ref[:, d + h * HEAD_DIM:d + (h + 1) * HEAD_DIM], kn_ref[...], 64), c64, s64)
        kf_ref[:, sl] = k
        kb_ref[:, sl] = k.astype(BF16)
        vt_ref[h] = raw_ref[:, 2 * d + h * HEAD_DIM:2 * d + (h + 1) * HEAD_DIM].T.astype(BF16)


def post_b(raw, tabs64, qn, kn, *, tm):
    grp, r, _ = raw.shape
    tm = min(tm, r)
    d = N_HEADS * HEAD_DIM
    rt = tabs64[0].shape[0]
    tab_spec = (pl.BlockSpec((1, LANES), lambda b, i: (0, 0)) if rt == 1
                else pl.BlockSpec((tm, LANES), lambda b, i: (i, 0)))
    vec_spec = pl.BlockSpec((1, LANES), lambda b, i: (0, 0))
    row = lambda w: pl.BlockSpec((None, tm, w), lambda b, i: (b, i, 0))
    return pl.pallas_call(
        _post_b_body,
        grid=(grp, r // tm),
        in_specs=[row(3 * d), tab_spec, tab_spec, vec_spec, vec_spec],
        out_specs=[pl.BlockSpec((None, 2, tm, d), lambda b, i: (b, 0, i, 0)), row(d), row(d),
                   pl.BlockSpec((None, N_HEADS, HEAD_DIM, tm), lambda b, i: (b, 0, 0, i))],
        out_shape=[jax.ShapeDtypeStruct((grp, 2, r, d), BF16), jax.ShapeDtypeStruct((grp, r, d), F32),
                   jax.ShapeDtypeStruct((grp, r, d), BF16), jax.ShapeDtypeStruct((grp, N_HEADS, HEAD_DIM, r), BF16)],
        compiler_params=_cparams(("arbitrary", "arbitrary")),
        name="post_diff",
    )(raw, tabs64[0], tabs64[1], qn, kn)


def _post_c_body(raw_ref, q_ref, kb_ref, vb_ref):
    d = N_HEADS * HEAD_DIM
    q_ref[...] = (raw_ref[:, 0:d] * HEAD_DIM ** -0.5).astype(BF16)
    kb_ref[...] = raw_ref[:, d:2 * d].astype(BF16)
    vb_ref[...] = raw_ref[:, 2 * d:3 * d].astype(BF16)


def post_c(raw, *, tm):
    grp, r, _ = raw.shape
    tm = min(tm, r)
    d = N_HEADS * HEAD_DIM
    row = lambda w: pl.BlockSpec((None, tm, w), lambda b, i: (b, i, 0))
    return pl.pallas_call(
        _post_c_body,
        grid=(grp, r // tm),
        in_specs=[row(3 * d)],
        out_specs=[row(d), row(d), row(d)],
        out_shape=[jax.ShapeDtypeStruct((grp, r, d), BF16)] * 3,
        compiler_params=_cparams(("arbitrary", "arbitrary")),
        name="post_sb",
    )(raw)


def _sort_key(score):
    bits = pltpu.bitcast(score, I32)
    bits = jnp.where(bits == INT_MIN, 0, bits)
    return jnp.where(bits < 0, bits ^ 0x7FFFFFFF, bits)


def _lane_total(cnt):
    tot = jnp.sum(cnt.astype(F32), axis=1, keepdims=True)
    return jnp.broadcast_to(tot, cnt.shape).astype(I32)


def _kth_largest(count_ge, rows, ktop, nbits=32):
    def step(i, cur):
        trial = cur + lax.shift_left(jnp.int32(1), nbits - 1 - i)
        return jnp.where(count_ge(trial) >= ktop, trial, cur)

    lowest = INT_MIN if nbits == 32 else -(1 << (nbits - 1))
    return lax.fori_loop(0, nbits, step, jnp.full((rows, LANES), lowest, I32))


def _tie_cutoff(count_eq_below, need, rows, nbits):
    def step(i, cur):
        trial = cur + lax.shift_left(jnp.int32(1), nbits - 1 - i)
        return jnp.where(count_eq_below(trial) < need, trial, cur)

    return lax.fori_loop(0, nbits, step, jnp.zeros((rows, LANES), I32))


def _dsa_select_body(q3_ref, wi_ref, k3_ref, bias_ref, key_ref, hi_ref, lo_ref, cut_ref, *, tq, ck, t_len, ktop):
    qb = pl.program_id(1)
    row0 = qb * tq
    nch = (row0 + tq - 1) // ck + 1
    nsub = ck // LANES
    q3 = q3_ref[...].reshape(IDX_HEADS * tq, 2 * LANES)
    wi = wi_ref[...]
    rows = row0 + lax.broadcasted_iota(I32, (tq, ck), 0)
    lane_cols = lax.broadcasted_iota(I32, (tq, ck), 1)
    half_min = -(1 << 15)

    def score_chunk(c, carry):
        c0 = pl.multiple_of(c * ck, ck)
        s = _dot_nt(q3, k3_ref[pl.ds(c0, ck), :])
        acc = jnp.zeros((tq, ck), F32)
        for h in range(IDX_HEADS):
            acc = acc + jnp.maximum(s[h * tq:(h + 1) * tq], 0.0) * wi[:, h:h + 1]
        key = jnp.where(c0 + lane_cols <= rows, _sort_key(acc), INT_MIN)
        key_ref[c] = key
        hi_ref[c] = lax.shift_right_arithmetic(key, 16).astype(I16)
        lo_ref[c] = ((key & 0xFFFF) + half_min).astype(I16)
        return carry

    lax.fori_loop(0, nch, score_chunk, 0)

    def count(ref, ind, zero):
        def body(c, cnt):
            m = ind(ref[c], c * ck + lane_cols)
            for j in range(nsub):
                cnt = cnt + m[:, j * LANES:(j + 1) * LANES]
            return cnt
        return _lane_total(lax.fori_loop(0, nch, body, jnp.zeros((tq, LANES), zero.dtype)))

    def wide(x):
        return jnp.concatenate([x] * nsub, axis=1)

    def wide16(x):
        return wide(x.astype(I16))

    one16, zero16 = jnp.int16(1), jnp.int16(0)
    thr_hi = _kth_largest(
        lambda trial: count(hi_ref, lambda v, cols: jnp.where(v >= wide16(trial), one16, zero16), zero16),
        tq, ktop, 16)
    hi_w = wide16(thr_hi)
    need_lo = ktop - count(hi_ref, lambda v, cols: jnp.where(v > hi_w, one16, zero16), zero16)

    def keep_bucket(c, carry):
        lo_ref[c] = jnp.where(hi_ref[c] == hi_w, lo_ref[c], jnp.int16(half_min))
        return carry

    lax.fori_loop(0, nch, keep_bucket, 0)
    thr_lo = _kth_largest(
        lambda trial: count(lo_ref, lambda v, cols: jnp.where(v >= wide16(trial), one16, zero16), zero16),
        tq, need_lo, 16)
    thr = thr_hi * 65536 + (thr_lo - half_min)
    thr_w = wide(thr)
    zero32 = jnp.int32(0)
    need = ktop - count(key_ref, lambda kc, cols: jnp.where(kc > thr_w, 1, 0), zero32)
    n_eq = count(key_ref, lambda kc, cols: jnp.where(kc == thr_w, 1, 0), zero32)
    cut_ref[...] = jnp.full((tq, LANES), t_len, I32)

    @pl.when(jnp.max(n_eq - need) > 0)
    def _():
        nbits = max(1, (t_len - 1).bit_length())
        cut_ref[...] = _tie_cutoff(
            lambda trial: count(
                key_ref, lambda kc, cols: jnp.where(kc == thr_w, jnp.where(cols < wide(trial), 1, 0), 0), zero32),
            need, tq, nbits)

    cut_w = wide(cut_ref[...])

    def write_chunk(c, carry):
        kc = key_ref[c]
        cols = c * ck + lane_cols
        tie = jnp.where(kc == thr_w, jnp.where(cols <= cut_w, 0.0, NEG), NEG)
        sel = jnp.where(kc > thr_w, 0.0, tie)
        bias_ref[c] = jnp.where(cols <= rows, sel, NEG).astype(BF16)
        return carry

    lax.fori_loop(0, nch, write_chunk, 0)

    def fill_chunk(c, carry):
        bias_ref[c] = jnp.full((tq, ck), NEG, BF16)
        return carry

    lax.fori_loop(nch, t_len // ck, fill_chunk, 0)


def dsa_select(q3, wi, k3, *, tq, ck):
    b, _, t_len, _ = q3.shape
    ktop = min(TOPK_MAX, t_len // 4)
    nck = t_len // ck
    return pl.pallas_call(
        functools.partial(_dsa_select_body, tq=tq, ck=ck, t_len=t_len, ktop=ktop),
        grid=(b, t_len // tq),
        in_specs=[
            pl.BlockSpec((None, IDX_HEADS, tq, 2 * LANES), lambda bb, i: (bb, 0, i, 0)),
            pl.BlockSpec((None, tq, IDX_HEADS), lambda bb, i: (bb, i, 0)),
            pl.BlockSpec((None, t_len, 2 * LANES), lambda bb, i: (bb, 0, 0)),
        ],
        out_specs=pl.BlockSpec((None, nck, tq, ck), lambda bb, i: (bb, 0, i, 0)),
        out_shape=jax.ShapeDtypeStruct((b, nck, t_len, ck), BF16),
        scratch_shapes=[pltpu.VMEM((nck, tq, ck), I32), pltpu.VMEM((nck, tq, ck), I16),
                        pltpu.VMEM((nck, tq, ck), I16), pltpu.VMEM((tq, LANES), I32)],
        compiler_params=_cparams(("arbitrary", "arbitrary")),
        name="dsa_select",
    )(q3, wi, k3)


def _kmax(qi, tq, bk):
    return ((qi + 1) * tq - 1) // bk


def _softmax_update_t(s_t, m_ref, l_ref, acc_ref, v_t, idx):
    m_old = m_ref[idx]
    m_new = jnp.maximum(m_old, jnp.max(s_t, axis=0, keepdims=True))
    alpha = jnp.exp(m_old - m_new)
    p = jnp.exp(s_t - m_new)
    l_ref[idx] = alpha * l_ref[idx] + jnp.sum(p, axis=0, keepdims=True)
    acc_ref[idx] = alpha * acc_ref[idx] + _dot(v_t, p.astype(BF16))
    m_ref[idx] = m_new


def _attn_init(m_ref, l_ref, acc_ref):
    m_ref[...] = jnp.full_like(m_ref, NEG)
    l_ref[...] = jnp.zeros_like(l_ref)
    acc_ref[...] = jnp.zeros_like(acc_ref)


def _dsa_attn_body(q_ref, k_ref, vt_ref, b_ref, o_ref, m_ref, l_ref, acc_ref, *, tq, bk, nk):
    qi, ki = pl.program_id(1), pl.program_id(2)

    @pl.when(ki == 0)
    def _():
        _attn_init(m_ref, l_ref, acc_ref)

    @pl.when(ki <= _kmax(qi, tq, bk))
    def _():
        bias_t = b_ref[...].astype(F32).T
        for h in range(N_HEADS):
            sl = slice(h * HEAD_DIM, (h + 1) * HEAD_DIM)
            s_t = _dot_nt(k_ref[:, sl], q_ref[:, sl]) + bias_t
            _softmax_update_t(s_t, m_ref, l_ref, acc_ref, vt_ref[h], h)

    @pl.when(ki == nk - 1)
    def _():
        for h in range(N_HEADS):
            sl = slice(h * HEAD_DIM, (h + 1) * HEAD_DIM)
            o_ref[:, sl] = (acc_ref[h] / l_ref[h]).T.astype(BF16)


def _kv_specs(tq, bk, d):
    k_spec = pl.BlockSpec((None, bk, d), lambda bb, i, j: (bb, jnp.minimum(j, _kmax(i, tq, bk)), 0))
    vt_spec = pl.BlockSpec((None, N_HEADS, HEAD_DIM, bk),
                           lambda bb, i, j: (bb, 0, 0, jnp.minimum(j, _kmax(i, tq, bk))))
    return k_spec, vt_spec


def dsa_attention(q, k, v_t, bias, *, tq):
    b, t_len, d = q.shape
    bk = bias.shape[3]
    nk = t_len // bk
    k_spec, vt_spec = _kv_specs(tq, bk, d)
    return pl.pallas_call(
        functools.partial(_dsa_attn_body, tq=tq, bk=bk, nk=nk),
        grid=(b, t_len // tq, nk),
        in_specs=[
            pl.BlockSpec((None, tq, d), lambda bb, i, j: (bb, i, 0)),
            k_spec, vt_spec,
            pl.BlockSpec((None, None, tq, bk), lambda bb, i, j: (bb, jnp.minimum(j, _kmax(i, tq, bk)), i, 0)),
        ],
        out_specs=pl.BlockSpec((None, tq, d), lambda bb, i, j: (bb, i, 0)),
        out_shape=jax.ShapeDtypeStruct((b, t_len, d), BF16),
        scratch_shapes=[pltpu.VMEM((N_HEADS, 1, tq), F32), pltpu.VMEM((N_HEADS, 1, tq), F32),
                        pltpu.VMEM((N_HEADS, HEAD_DIM, tq), F32)],
        compiler_params=_cparams(("arbitrary", "arbitrary", "arbitrary")),
        name="dsa_attention",
    )(q, k, v_t, bias)


def _diff_lambda(lam_ref, lam_init):
    lp = lam_ref[...]
    a = jnp.sum(lp[0:1] * lp[1:2], axis=-1, keepdims=True)
    b = jnp.sum(lp[2:3] * lp[3:4], axis=-1, keepdims=True)
    return jnp.exp(a) - jnp.exp(b) + lam_init


def _diff_attn_body(q2_ref, k_ref, vt_ref, lam_ref, sub_ref, o_ref, m_ref, l_ref, acc_ref,
                    *, tq, bk, nk, lam_init):
    qi, ki = pl.program_id(1), pl.program_id(2)
    kmax = _kmax(qi, tq, bk)

    @pl.when(ki == 0)
    def _():
        _attn_init(m_ref, l_ref, acc_ref)

    def step(masked):
        q2 = q2_ref[...].reshape(2 * tq, N_HEADS * HEAD_DIM)
        if masked:
            keys = ki * bk + lax.broadcasted_iota(I32, (bk, 2 * tq), 0)
            queries = qi * tq + lax.broadcasted_iota(I32, (bk, 2 * tq), 1) % tq
            allowed = keys <= queries
        for h in range(N_HEADS):
            sl = slice(h * HEAD_DIM, (h + 1) * HEAD_DIM)
            s_t = _dot_nt(k_ref[:, sl], q2[:, sl])
            if masked:
                s_t = jnp.where(allowed, s_t, NEG)
            _softmax_update_t(s_t, m_ref, l_ref, acc_ref, vt_ref[h], h)

    @pl.when((ki + 1) * bk - 1 <= qi * tq)
    def _():
        step(False)

    @pl.when(jnp.logical_and((ki + 1) * bk - 1 > qi * tq, ki <= kmax))
    def _():
        step(True)

    @pl.when(ki == nk - 1)
    def _():
        lam = _diff_lambda(lam_ref, lam_init)
        for h in range(N_HEADS):
            sl = slice(h * HEAD_DIM, (h + 1) * HEAD_DIM)
            o_all = acc_ref[h] / l_ref[h]
            o = o_all[:, 0:tq] - lam * o_all[:, tq:2 * tq]
            ms = jnp.mean(o * o, axis=0, keepdims=True)
            o = o * lax.rsqrt(ms + NORM_EPS) * sub_ref[...] * (1.0 - lam_init)
            o_ref[:, sl] = o.T.astype(BF16)


def diff_attention(q2, k, v_t, lam_params, subln_col, lam_init, *, tq, bk):
    b, _, t_len, d = q2.shape
    nk = t_len // bk
    k_spec, vt_spec = _kv_specs(tq, bk, d)
    return pl.pallas_call(
        functools.partial(_diff_attn_body, tq=tq, bk=bk, nk=nk, lam_init=lam_init),
        grid=(b, t_len // tq, nk),
        in_specs=[
            pl.BlockSpec((None, 2, tq, d), lambda bb, i, j: (bb, 0, i, 0)),
            k_spec, vt_spec,
            pl.BlockSpec(lam_params.shape, lambda bb, i, j: (0, 0)),
            pl.BlockSpec((HEAD_DIM, 1), lambda bb, i, j: (0, 0)),
        ],
        out_specs=pl.BlockSpec((None, tq, d), lambda bb, i, j: (bb, i, 0)),
        out_shape=jax.ShapeDtypeStruct((b, t_len, d), BF16),
        scratch_shapes=[pltpu.VMEM((N_HEADS, 1, 2 * tq), F32), pltpu.VMEM((N_HEADS, 1, 2 * tq), F32),
                        pltpu.VMEM((N_HEADS, HEAD_DIM, 2 * tq), F32)],
        compiler_params=_cparams(("arbitrary", "arbitrary", "arbitrary")),
        name="diff_attention",
    )(q2, k, v_t, lam_params, subln_col)


def _softplus(z):
    return jnp.maximum(z, 0.0) + jnp.log1p(jnp.exp(-jnp.abs(z)))


def _sb_attn_body(q_ref, k_ref, v_ref, u_ref, o_ref, c_ref, acc_ref, done_ref, *, tq, bk, sub, nk):
    qi, ki = pl.program_id(1), pl.program_id(2)
    kb = _kmax(qi, tq, bk) - ki

    @pl.when(ki == 0)
    def _():
        c_ref[...] = jnp.zeros_like(c_ref)
        acc_ref[...] = jnp.zeros_like(acc_ref)
        done_ref[0] = 0

    @pl.when(jnp.logical_and(kb >= 0, done_ref[0] == 0))
    def _():
        rows = qi * tq + lax.broadcasted_iota(I32, (tq, sub), 0)
        lane_cols = lax.broadcasted_iota(I32, (tq, sub), 1)
        u = u_ref[...]
        for h in range(N_HEADS):
            sl = slice(h * HEAD_DIM, (h + 1) * HEAD_DIM)
            crun = c_ref[h]
            q = q_ref[:, sl]
            for j in reversed(range(bk // sub)):
                ks = slice(j * sub, (j + 1) * sub)
                z = _dot_nt(q, k_ref[ks, sl])
                mask = kb * bk + j * sub + lane_cols < rows
                sp = _softplus(z)
                lk = jnp.where(mask, -sp, 0.0)
                lk_hi, lk_lo = _split_bf16(lk)
                after = crun + _dot(lk_hi, u) + _dot(lk_lo, u)
                a = jnp.where(mask, jnp.exp(after + z - sp), 0.0)
                acc_ref[:, sl] += _dot(a.astype(BF16), v_ref[ks, sl])
                crun = crun + jnp.sum(lk, axis=-1, keepdims=True)
            c_ref[h] = crun
        done_ref[0] = jnp.where(jnp.max(c_ref[...]) < SB_EXP_UNDERFLOW, 1, 0)

    @pl.when(ki == nk - 1)
    def _():
        o_ref[...] = acc_ref[...].astype(BF16)


def _later_key_matrix(n):
    j = lax.broadcasted_iota(I32, (n, n), 0)
    s = lax.broadcasted_iota(I32, (n, n), 1)
    return (j > s).astype(BF16)


def sb_attention(q, k, v, *, tq, bk, sub):
    b, t_len, d = q.shape
    nk = t_len // bk
    kv_spec = pl.BlockSpec((None, bk, d), lambda bb, i, j: (bb, jnp.maximum(_kmax(i, tq, bk) - j, 0), 0))
    return pl.pallas_call(
        functools.partial(_sb_attn_body, tq=tq, bk=bk, sub=sub, nk=nk),
        grid=(b, t_len // tq, nk),
        in_specs=[
            pl.BlockSpec((None, tq, d), lambda bb, i, j: (bb, i, 0)),
            kv_spec, kv_spec,
            pl.BlockSpec((sub, sub), lambda bb, i, j: (0, 0)),
        ],
        out_specs=pl.BlockSpec((None, tq, d), lambda bb, i, j: (bb, i, 0)),
        out_shape=jax.ShapeDtypeStruct((b, t_len, d), BF16),
        scratch_shapes=[pltpu.VMEM((N_HEADS, tq, 1), F32), pltpu.VMEM((tq, d), F32), pltpu.SMEM((1,), I32)],
        compiler_params=_cparams(("arbitrary", "arbitrary", "arbitrary")),
        name="sb_attention",
    )(q, k, v, _later_key_matrix(sub))


def _sample_scores_body(pt_ref, qh_ref, ql_ref, wi_ref, kn_ref, *rest, n_pages):
    page_refs, out_ref = rest[:n_pages], rest[n_pages]
    qh, ql = qh_ref[...], ql_ref[...]
    w = wi_ref[...]
    for p in range(n_pages):
        k_hi, k_lo = _split_bf16(page_refs[p][...])
        s = _dot_nt(qh, k_hi) + _dot_nt(ql, k_hi) + _dot_nt(qh, k_lo)
        out_ref[p:p + 1, :] = jnp.sum(jnp.maximum(s, 0.0) * w, axis=0, keepdims=True)
    q = qh.astype(F32) + ql.astype(F32)
    s_new = jnp.sum(q * kn_ref[...], axis=-1, keepdims=True)
    i_new = jnp.sum(jnp.maximum(s_new, 0.0) * w, axis=0, keepdims=True)
    lane = lax.broadcasted_iota(I32, (1, LANES), 1)
    out_ref[n_pages:n_pages + 1, :] = jnp.where(lane == 0, i_new, -jnp.inf)


def sample_scores(page_table, qh, ql, wi_col, ki_new, pool_kidx, layer):
    db, n_pages = page_table.shape
    page_specs = [
        pl.BlockSpec((None, None, PAGE_SIZE, IDX_DIM),
                     functools.partial(lambda r, pt, p: (layer, pt[r, p], 0, 0), p=p))
        for p in range(n_pages)]
    grid_spec = pltpu.PrefetchScalarGridSpec(
        num_scalar_prefetch=1,
        grid=(db,),
        in_specs=[
            pl.BlockSpec((None, IDX_HEADS, IDX_DIM), lambda r, pt: (r, 0, 0)),
            pl.BlockSpec((None, IDX_HEADS, IDX_DIM), lambda r, pt: (r, 0, 0)),
            pl.BlockSpec((None, IDX_HEADS, 1), lambda r, pt: (r, 0, 0)),
            pl.BlockSpec((None, 1, IDX_DIM), lambda r, pt: (r, 0, 0)),
        ] + page_specs,
        out_specs=pl.BlockSpec((None, n_pages + 1, LANES), lambda r, pt: (r, 0, 0)),
    )
    return pl.pallas_call(
        functools.partial(_sample_scores_body, n_pages=n_pages),
        grid_spec=grid_spec,
        out_shape=jax.ShapeDtypeStruct((db, n_pages + 1, LANES), F32),
        compiler_params=_cparams(("arbitrary",)),
        name="sample_scores",
    )(page_table, qh, ql, wi_col, ki_new, *([pool_kidx] * n_pages))


def _sample_select_body(sc_ref, e_ref, bias_ref, *, db, n_chunks, ktop):
    lane = lax.broadcasted_iota(I32, (db, LANES), 1)
    keys = []
    for c in range(n_chunks):
        s = sc_ref[:, c * LANES:(c + 1) * LANES]
        keys.append(jnp.where(s == -jnp.inf, INT_MIN, _sort_key(s)))

    def count(ind):
        cnt = jnp.zeros((db, LANES), I32)
        for c in range(n_chunks):
            cnt = cnt + ind(keys[c], c * LANES + lane)
        return _lane_total(cnt)

    thr = _kth_largest(lambda trial: count(lambda kc, cols: jnp.where(kc >= trial, 1, 0)), db, ktop)
    need = ktop - count(lambda kc, cols: jnp.where(kc > thr, 1, 0))
    nbits = max(1, (n_chunks * LANES - 1).bit_length())
    cut = _tie_cutoff(
        lambda trial: count(lambda kc, cols: jnp.where(kc == thr, jnp.where(cols < trial, 1, 0), 0)),
        need, db, nbits)
    e = e_ref[...]
    for c in range(n_chunks):
        cols = c * LANES + lane
        tie = jnp.where(keys[c] == thr, jnp.where(cols <= cut, 1.0, 0.0), 0.0)
        sel = jnp.where(keys[c] > thr, 1.0, tie)
        sel = jnp.where(keys[c] == INT_MIN, 0.0, sel)
        wide = _dot(sel.astype(BF16), e)
        bias_ref[c] = jnp.where(wide > 0.5, 0.0, NEG)


def sample_select(scores2d, ktop):
    db, n = scores2d.shape
    n_chunks = n // LANES
    flat = N_HEADS * PAGE_SIZE
    t_of_lane = lax.broadcasted_iota(I32, (PAGE_SIZE, flat), 1) // N_HEADS
    e = (t_of_lane == lax.broadcasted_iota(I32, (PAGE_SIZE, flat), 0)).astype(BF16)
    return pl.pallas_call(
        functools.partial(_sample_select_body, db=db, n_chunks=n_chunks, ktop=ktop),
        grid=(1,),
        in_specs=[pl.BlockSpec((db, n), lambda i: (0, 0)), pl.BlockSpec((PAGE_SIZE, flat), lambda i: (0, 0))],
        out_specs=pl.BlockSpec((n_chunks, db, flat), lambda i: (0, 0, 0)),
        out_shape=jax.ShapeDtypeStruct((n_chunks, db, flat), F32),
        compiler_params=_cparams(("arbitrary",)),
        name="sample_select",
    )(scores2d, e)


def _suffix_sum_stride8(x):
    n = x.shape[1]
    lane = lax.broadcasted_iota(I32, x.shape, 1)
    sh = N_HEADS
    while sh < n:
        x = x + jnp.where(lane < n - sh, pltpu.roll(x, n - sh, 1), 0.0)
        sh *= 2
    return x


def _sample_attn_body(pt_ref, q_ref, kn_ref, vn_ref, *rest, mode, pps, n_steps, lam_init):
    k_refs, v_refs = rest[:pps], rest[pps:2 * pps]
    rest = rest[2 * pps:]
    if mode == "dsa":
        bias_ref, bnew_ref, o_ref, m_ref, l_ref, acc_ref = rest
    elif mode == "diff":
        lam_ref, sub_ref, o_ref, m_ref, l_ref, acc_ref = rest
    else:
        o_ref, m_ref, l_ref, acc_ref = rest
    s_id = pl.program_id(1)
    nrow = q_ref.shape[0]
    flat = N_HEADS * PAGE_SIZE
    q = q_ref[...]
    valid = (lax.broadcasted_iota(I32, (nrow, flat), 1) % N_HEADS
             == lax.broadcasted_iota(I32, (nrow, flat), 0) % N_HEADS)

    @pl.when(s_id == 0)
    def _():
        m_ref[...] = jnp.full_like(m_ref, 0.0 if mode == "sb" else NEG)
        l_ref[...] = jnp.zeros_like(l_ref)
        acc_ref[...] = jnp.zeros_like(acc_ref)

    order = reversed(range(pps)) if mode == "sb" else range(pps)
    for i in order:
        k2d = k_refs[i][...].reshape(flat, HEAD_DIM).astype(BF16)
        v2d = v_refs[i][...].reshape(flat, HEAD_DIM).astype(BF16)
        s_all = _dot_nt(q, k2d)
        if mode == "sb":
            sp = _softplus(s_all)
            lk = jnp.where(valid, -sp, 0.0)
            incl = _suffix_sum_stride8(lk)
            after = m_ref[...] + (incl - lk)
            a = jnp.where(valid, jnp.exp(after + s_all - sp), 0.0)
            acc_ref[...] += _dot(a.astype(BF16), v2d)
            m_ref[...] += jnp.sum(lk, axis=-1, keepdims=True)
        else:
            s = jnp.where(valid, s_all, NEG)
            if mode == "dsa":
                s = s + bias_ref[i:i + 1, :]
            m_old = m_ref[...]
            m_new = jnp.maximum(m_old, jnp.max(s, axis=-1, keepdims=True))
            alpha = jnp.exp(m_old - m_new)
            p = jnp.where(valid, jnp.exp(s - m_new), 0.0)
            l_ref[...] = alpha * l_ref[...] + jnp.sum(p, axis=-1, keepdims=True)
            acc_ref[...] = alpha * acc_ref[...] + _dot(p.astype(BF16), v2d)
            m_ref[...] = m_new

    @pl.when(s_id == n_steps - 1)
    def _():
        reps = nrow // N_HEADS
        k_new = jnp.concatenate([kn_ref[...]] * reps, axis=0)
        v_new = jnp.concatenate([vn_ref[...]] * reps, axis=0)
        if mode == "sb":
            o_ref[...] = acc_ref[...] + jnp.zeros((nrow, 1), F32) * v_new
            return
        s_new = jnp.sum(q.astype(F32) * k_new, axis=-1, keepdims=True)
        if mode == "dsa":
            s_new = s_new + bnew_ref[:, 0:1]
        m_old = m_ref[...]
        m_new = jnp.maximum(m_old, s_new)
        alpha = jnp.exp(m_old - m_new)
        p_new = jnp.exp(s_new - m_new)
        l_fin = alpha * l_ref[...] + p_new
        o = (alpha * acc_ref[...] + p_new * v_new) / l_fin
        if mode == "diff":
            lam = _diff_lambda(lam_ref, lam_init)
            o = o[0:N_HEADS] - lam * o[N_HEADS:2 * N_HEADS]
            ms = jnp.mean(o * o, axis=-1, keepdims=True)
            o = o * lax.rsqrt(ms + NORM_EPS) * sub_ref[...] * (1.0 - lam_init)
        o_ref[...] = o


def sample_attention(mode, page_table, q_rows, k_new, v_new, pool_k, pool_v, layer, *, pps,
                     bias_pages=None, bias_new=None, lam_params=None, subln=None, lam_init=0.0):
    db, n_pages = page_table.shape
    nrow = q_rows.shape[1]
    n_steps = n_pages // pps
    flat = N_HEADS * PAGE_SIZE

    def page_map(i):
        if mode == "sb":
            return lambda r, s, pt: (layer, pt[r, (n_steps - 1 - s) * pps + i], 0, 0, 0)
        return lambda r, s, pt: (layer, pt[r, s * pps + i], 0, 0, 0)

    page_specs = [pl.BlockSpec((None, None, PAGE_SIZE, N_HEADS, HEAD_DIM), page_map(i)) for i in range(pps)]
    in_specs = [
        pl.BlockSpec((None, nrow, HEAD_DIM), lambda r, s, pt: (r, 0, 0)),
        pl.BlockSpec((None, N_HEADS, HEAD_DIM), lambda r, s, pt: (r, 0, 0)),
        pl.BlockSpec((None, N_HEADS, HEAD_DIM), lambda r, s, pt: (r, 0, 0)),
    ] + page_specs + page_specs
    args = [q_rows, k_new, v_new] + [pool_k] * pps + [pool_v] * pps
    if mode == "dsa":
        in_specs += [pl.BlockSpec((None, None, pps, flat), lambda r, s, pt: (r, s, 0, 0)),
                     pl.BlockSpec((None, 1, flat), lambda r, s, pt: (r, 0, 0))]
        args += [bias_pages, bias_new]
    elif mode == "diff":
        in_specs += [pl.BlockSpec(lam_params.shape, lambda r, s, pt: (0, 0)),
                     pl.BlockSpec((1, HEAD_DIM), lambda r, s, pt: (0, 0))]
        args += [lam_params, subln]
    grid_spec = pltpu.PrefetchScalarGridSpec(
        num_scalar_prefetch=1,
        grid=(db, n_steps),
        in_specs=in_specs,
        out_specs=pl.BlockSpec((None, N_HEADS, HEAD_DIM), lambda r, s, pt: (r, 0, 0)),
        scratch_shapes=[pltpu.VMEM((nrow, 1), F32), pltpu.VMEM((nrow, 1), F32),
                        pltpu.VMEM((nrow, HEAD_DIM), F32)],
    )
    return pl.pallas_call(
        functools.partial(_sample_attn_body, mode=mode, pps=pps, n_steps=n_steps, lam_init=lam_init),
        grid_spec=grid_spec,
        out_shape=jax.ShapeDtypeStruct((db, N_HEADS, HEAD_DIM), F32),
        compiler_params=_cparams(("arbitrary", "arbitrary")),
        name="sample_attention_" + mode,
    )(page_table, *args)


def _idx_weight(w_in):
    d = w_in.shape[0]
    hd = N_HEADS * HEAD_DIM
    base = 3 * hd
    cols = []
    zeros64 = jnp.zeros((d, LANES - IDX_DIM), w_in.dtype)
    for h in range(IDX_HEADS):
        cols += [w_in[:, base + h * IDX_DIM:base + (h + 1) * IDX_DIM], zeros64]
    cols += [w_in[:, base + IDX_HEADS * IDX_DIM:base + IDX_HEADS * IDX_DIM + IDX_DIM], zeros64]
    wi = w_in[:, base + IDX_HEADS * IDX_DIM + IDX_DIM:]
    cols += [wi, jnp.zeros((d, LANES - wi.shape[1]), w_in.dtype)]
    return jnp.concatenate(cols, axis=1)


def kernel(x_prompt, x_sample, cache_a_k, cache_a_v, cache_a_kidx, cache_b_k, cache_b_v, cache_c_k, cache_c_v, page_table, c_prompt, c_sample, ada_w, ada_b, norm_mix, norm_ffn, ffn_w1, ffn_w2, a_w_in, a_q_norm, a_k_norm, a_idx_k_norm, a_w_out, b_w_in, b_q_norm, b_k_norm, b_lambda, b_subln, b_w_out, c_w_in, c_w_out):
    b, t_len, d = x_prompt.shape
    db = x_sample.shape[0]
    depth = ada_w.shape[0]
    n_pages = page_table.shape[1]
    past = n_pages * PAGE_SIZE
    hd = (N_HEADS, HEAD_DIM)

    n_c = b + db
    n_c_pad = -(-n_c // 8) * 8
    c_all = jnp.concatenate([c_prompt, c_sample, jnp.zeros((n_c_pad - n_c, d), F32)], axis=0)
    mod = ada_modulation_all(c_all, ada_w, ada_b)

    tabs128_p = rope_tables(0, t_len, 64)
    tabs64_p = rope_tables(0, t_len, 32)
    tabs128_s = tuple(x[0:1] for x in rope_tables(past, 8, 64))
    tabs64_s = tuple(x[0:1] for x in rope_tables(past, 8, 32))

    xp = x_prompt
    xs = x_sample.reshape(1, db, d)
    rows = {name: [] for name in ("a_k", "a_v", "a_kidx", "b_k", "b_v", "c_k", "c_v")}
    vec = lambda a: a.reshape(1, -1)

    for i in range(depth):
        kind, j = i % N_MIXERS, i // N_MIXERS
        mp = [mod[i, :b, k * d:(k + 1) * d].reshape(b, 1, d) for k in range(6)]
        ms = [mod[i, b:n_c, k * d:(k + 1) * d].reshape(1, db, d) for k in range(6)]
        g_mix, g_ffn = vec(norm_mix[i]), vec(norm_ffn[i])

        if kind == 0:
            w_in = a_w_in[j]
            w_qkv = (w_in[:, :3 * d].astype(BF16),)
            w_idx = _split_bf16(_idx_weight(w_in))
            qn, kn = vec(a_q_norm[j]), vec(a_k_norm[j])
            ikn = jnp.pad(a_idx_k_norm[j], (0, LANES - IDX_DIM)).reshape(1, LANES)
            w_out = a_w_out[j].astype(BF16)

            raw = proj(xp, g_mix, mp[0], mp[1], w_qkv, tm=512, tn=1024, name="proj_dsa")
            idx = proj(xp, g_mix, mp[0], mp[1], w_idx, tm=512, tn=w_idx[0].shape[1], name="proj_dsa_idx")
            q_bf, k_f, k_bf, v_t, q3, ki_f, k3, wi = post_a(raw, idx, tabs128_p, tabs64_p, qn, kn, ikn, tm=256)
            bias = dsa_select(q3, wi, k3, tq=256, ck=512)
            op = dsa_attention(q_bf, k_bf, v_t, bias, tq=512)
            kp, vp, kip = k_f.reshape(b, t_len, *hd), raw[:, :, 2 * d:].reshape(b, t_len, *hd), ki_f

            raw_s = proj(xs, g_mix, ms[0], ms[1], w_qkv, tm=128, tn=1024, name="proj_dsa")
            idx_s = proj(xs, g_mix, ms[0], ms[1], w_idx, tm=128, tn=w_idx[0].shape[1], name="proj_dsa_idx")
            qs_bf, ks_f, _, _, q3_s, kis_f, _, wi_s = post_a(raw_s, idx_s, tabs128_s, tabs64_s, qn, kn, ikn, tm=128)
            qh = jnp.transpose(q3_s[0, :, :, 0:IDX_DIM], (1, 0, 2))
            ql = jnp.transpose(q3_s[0, :, :, IDX_DIM:2 * IDX_DIM], (1, 0, 2))
            scores = sample_scores(page_table, qh, ql, wi_s.reshape(db, IDX_HEADS, 1),
                                   kis_f.reshape(db, 1, IDX_DIM), cache_a_kidx, j)
            ktop = min(TOPK_MAX, (past + 1) // 4)
            sel_bias = sample_select(scores.reshape(db, (n_pages + 1) * LANES), ktop)
            pps = 8
            bias_pages = jnp.transpose(sel_bias[:n_pages], (1, 0, 2)).reshape(db, n_pages // pps, pps, -1)
            bias_new = jnp.transpose(sel_bias[n_pages:], (1, 0, 2))
            ks_, vs_ = ks_f.reshape(db, *hd), raw_s[0, :, 2 * d:].reshape(db, *hd)
            os_ = sample_attention("dsa", page_table, qs_bf.reshape(db, *hd), ks_, vs_, cache_a_k, cache_a_v, j,
                                   pps=pps, bias_pages=bias_pages, bias_new=bias_new)
            rows["a_k"].append((kp, ks_))
            rows["a_v"].append((vp, vs_))
            rows["a_kidx"].append((kip, kis_f.reshape(db, 1, IDX_DIM)))
        elif kind == 1:
            lam_init = 0.8 - 0.6 * math.exp(-0.3 * i)
            w_in = (b_w_in[j].astype(BF16),)
            qn = vec(jnp.tile(b_q_norm[j], 2))
            kn = vec(jnp.tile(b_k_norm[j], 2))
            subln = vec(b_subln[j])
            w_out = b_w_out[j].astype(BF16)

            raw = proj(xp, g_mix, mp[0], mp[1], w_in, tm=512, tn=1024, name="proj_diff")
            q2, k_f, k_bf, v_t = post_b(raw, tabs64_p, qn, kn, tm=256)
            op = diff_attention(q2, k_bf, v_t, b_lambda[j], subln.reshape(HEAD_DIM, 1), lam_init, tq=512, bk=512)
            kp, vp = k_f.reshape(b, t_len, *hd), raw[:, :, 2 * d:].reshape(b, t_len, *hd)

            raw_s = proj(xs, g_mix, ms[0], ms[1], w_in, tm=128, tn=1024, name="proj_diff")
            q2_s, ks_f, _, _ = post_b(raw_s, tabs64_s, qn, kn, tm=128)
            q_rows = jnp.transpose(q2_s[0].reshape(2, db, *hd), (1, 0, 2, 3)).reshape(db, 2 * N_HEADS, HEAD_DIM)
            ks_, vs_ = ks_f.reshape(db, *hd), raw_s[0, :, 2 * d:].reshape(db, *hd)
            os_ = sample_attention("diff", page_table, q_rows, ks_, vs_, cache_b_k, cache_b_v, j, pps=8,
                                   lam_params=b_lambda[j], subln=subln, lam_init=lam_init)
            rows["b_k"].append((kp, ks_))
            rows["b_v"].append((vp, vs_))
        else:
            w_in = (c_w_in[j].astype(BF16),)
            w_out = c_w_out[j].astype(BF16)

            raw = proj(xp, g_mix, mp[0], mp[1], w_in, tm=512, tn=1024, name="proj_sb")
            q_bf, k_bf, v_bf = post_c(raw, tm=256)
            op = sb_attention(q_bf, k_bf, v_bf, tq=512, bk=512, sub=256)
            kp, vp = raw[:, :, d:2 * d].reshape(b, t_len, *hd), raw[:, :, 2 * d:].reshape(b, t_len, *hd)

            raw_s = proj(xs, g_mix, ms[0], ms[1], w_in, tm=128, tn=1024, name="proj_sb")
            qs_bf, _, _ = post_c(raw_s, tm=128)
            ks_, vs_ = raw_s[0, :, d:2 * d].reshape(db, *hd), raw_s[0, :, 2 * d:].reshape(db, *hd)
            os_ = sample_attention("sb", page_table, qs_bf.reshape(db, *hd), ks_, vs_, cache_c_k, cache_c_v, j, pps=8)
            rows["c_k"].append((kp, ks_))
            rows["c_v"].append((vp, vs_))

        w1, w2 = ffn_w1[i].astype(BF16), ffn_w2[i].astype(BF16)
        xp = out_proj_residual(op, w_out, xp, mp[2], tm=512)
        xp = mlp_residual(xp, g_ffn, mp[3], mp[4], mp[5], w1, w2, tm=1024, tf=1024)
        xs = out_proj_residual(os_.reshape(1, db, d).astype(BF16), w_out, xs, ms[2], tm=128)
        xs = mlp_residual(xs, g_ffn, ms[3], ms[4], ms[5], w1, w2, tm=128, tf=1024)

    def stack(name, which):
        return jnp.stack([r[which] for r in rows[name]])

    outs = [xp, xs.reshape(db, 1, d)]
    for which in (0, 1):
        for name in ("a_k", "a_v", "a_kidx", "b_k", "b_v", "c_k", "c_v"):
            y = stack(name, which)
            if which == 1 and name != "a_kidx":
                y = y.reshape(y.shape[0], db, 1, *hd)
            outs.append(y)
    return tuple(outs)
```

```python
import functools
import math

import jax
import jax.numpy as jnp
from jax import lax
from jax.experimental import pallas as pl
from jax.experimental.pallas import tpu as pltpu

F32 = jnp.float32
BF16 = jnp.bfloat16
I32 = jnp.int32

D_MODEL = 1024
N_HEADS = 8
HEAD_DIM = 128
IDX_HEADS = 8
IDX_DIM = 64
TOPK_MAX = 256
PAGE_SIZE = 128
ROPE_THETA = 10000.0
NORM_EPS = 1e-6
N_MIXERS = 3

LANES = 128
NEG = -1e30
SB_EXP_UNDERFLOW = -110.0
INT_MIN = -2147483648
VMEM_LIMIT = 56 * 1024 * 1024
NT_DIMS = (((1,), (1,)), ((), ()))


def _cparams(sem):
    return pltpu.CompilerParams(dimension_semantics=sem, vmem_limit_bytes=VMEM_LIMIT)


def _split_bf16(x):
    hi = x.astype(BF16)
    lo = (x - hi.astype(F32)).astype(BF16)
    return hi, lo


def _dot(a, b):
    return jnp.dot(a, b, preferred_element_type=F32)


def _dot_nt(a, b):
    return lax.dot_general(a, b, NT_DIMS, preferred_element_type=F32)


def _norm_mod(x, g, sc, sh):
    ms = jnp.mean(x * x, axis=-1, keepdims=True)
    return (x * lax.rsqrt(ms + NORM_EPS) * g) * (1.0 + sc) + sh


def _mod_spec(rm, tm, d):
    if rm == 1:
        return pl.BlockSpec((None, 1, d), lambda g, i, j: (g, 0, 0))
    return pl.BlockSpec((None, tm, d), lambda g, i, j: (g, i, 0))


def _ada_body(c_ref, w_ref, b_ref, o_ref):
    c = c_ref[...]
    s = c * jax.nn.sigmoid(c)
    s_hi, s_lo = _split_bf16(s)
    w_hi, w_lo = _split_bf16(w_ref[...])
    o_ref[...] = _dot(s_hi, w_hi) + _dot(s_lo, w_hi) + _dot(s_hi, w_lo) + b_ref[...]


def ada_modulation_all(c_all, ada_w, ada_b):
    depth, d, n = ada_w.shape
    r = c_all.shape[0]
    tn = 1536
    return pl.pallas_call(
        _ada_body,
        grid=(depth, n // tn),
        in_specs=[
            pl.BlockSpec((r, d), lambda i, j: (0, 0)),
            pl.BlockSpec((None, d, tn), lambda i, j: (i, 0, j)),
            pl.BlockSpec((None, 1, tn), lambda i, j: (i, 0, j)),
        ],
        out_specs=pl.BlockSpec((None, r, tn), lambda i, j: (i, 0, j)),
        out_shape=jax.ShapeDtypeStruct((depth, r, n), F32),
        compiler_params=_cparams(("arbitrary", "arbitrary")),
        name="ada_modulation",
    )(c_all, ada_w, ada_b.reshape(depth, 1, n))


def _rope_tab_body(invf_ref, sgn_ref, cos_ref, sin_ref, *, start, tr):
    i = pl.program_id(0)
    pos = (start + i * tr + lax.broadcasted_iota(I32, (tr, LANES), 0)).astype(F32)
    ang = pos * invf_ref[...]
    cos_ref[...] = jnp.cos(ang)
    sin_ref[...] = jnp.sin(ang) * sgn_ref[...]


def rope_tables(start, rows, half):
    inv_freq = ROPE_THETA ** (-jnp.arange(half, dtype=F32) / half)
    reps = LANES // (2 * half)
    invf = jnp.tile(jnp.concatenate([inv_freq, inv_freq]), reps).reshape(1, LANES)
    sgn = jnp.tile(jnp.concatenate([-jnp.ones((half,), F32), jnp.ones((half,), F32)]), reps).reshape(1, LANES)
    tr = min(rows, 512)
    return pl.pallas_call(
        functools.partial(_rope_tab_body, start=start, tr=tr),
        grid=(rows // tr,),
        in_specs=[pl.BlockSpec((1, LANES), lambda i: (0, 0)), pl.BlockSpec((1, LANES), lambda i: (0, 0))],
        out_specs=[pl.BlockSpec((tr, LANES), lambda i: (i, 0)), pl.BlockSpec((tr, LANES), lambda i: (i, 0))],
        out_shape=[jax.ShapeDtypeStruct((rows, LANES), F32)] * 2,
        compiler_params=_cparams(("arbitrary",)),
        name="rope_tables",
    )(invf, sgn)


def _proj_body(x_ref, sh_ref, sc_ref, g_ref, w_ref, o_ref, hn_ref):
    @pl.when(pl.program_id(2) == 0)
    def _():
        hn_ref[...] = _norm_mod(x_ref[...], g_ref[...], sc_ref[...], sh_ref[...]).astype(BF16)

    o_ref[...] = _dot(hn_ref[...], w_ref[...])


def _proj3_body(x_ref, sh_ref, sc_ref, g_ref, whi_ref, wlo_ref, o_ref, hhi_ref, hlo_ref):
    @pl.when(pl.program_id(2) == 0)
    def _():
        hi, lo = _split_bf16(_norm_mod(x_ref[...], g_ref[...], sc_ref[...], sh_ref[...]))
        hhi_ref[...] = hi
        hlo_ref[...] = lo

    o_ref[...] = (_dot(hhi_ref[...], whi_ref[...]) + _dot(hlo_ref[...], whi_ref[...])
                  + _dot(hhi_ref[...], wlo_ref[...]))


def proj(x, g, sh, sc, ws, *, tm, tn, name):
    grp, r, d = x.shape
    n = ws[0].shape[1]
    tm = min(tm, r)
    precise = len(ws) == 2
    w_spec = pl.BlockSpec((d, tn), lambda b, i, j: (0, j))
    return pl.pallas_call(
        _proj3_body if precise else _proj_body,
        grid=(grp, r // tm, n // tn),
        in_specs=[
            pl.BlockSpec((None, tm, d), lambda b, i, j: (b, i, 0)),
            _mod_spec(sh.shape[1], tm, d),
            _mod_spec(sc.shape[1], tm, d),
            pl.BlockSpec((1, d), lambda b, i, j: (0, 0)),
        ] + [w_spec] * len(ws),
        out_specs=pl.BlockSpec((None, tm, tn), lambda b, i, j: (b, i, j)),
        out_shape=jax.ShapeDtypeStruct((grp, r, n), F32),
        scratch_shapes=[pltpu.VMEM((tm, d), BF16)] * len(ws),
        compiler_params=_cparams(("arbitrary", "arbitrary", "arbitrary")),
        name=name,
    )(x, sh, sc, g, *ws)


def _oproj_body(o_ref, w_ref, x_ref, gt_ref, y_ref):
    y_ref[...] = x_ref[...] + gt_ref[...] * _dot(o_ref[...], w_ref[...])


def out_proj_residual(o, w, x, gate, *, tm):
    grp, r, d = x.shape
    tm = min(tm, r)
    return pl.pallas_call(
        _oproj_body,
        grid=(grp, r // tm, 1),
        in_specs=[
            pl.BlockSpec((None, tm, d), lambda b, i, j: (b, i, 0)),
            pl.BlockSpec((d, d), lambda b, i, j: (0, 0)),
            pl.BlockSpec((None, tm, d), lambda b, i, j: (b, i, 0)),
            _mod_spec(gate.shape[1], tm, d),
        ],
        out_specs=pl.BlockSpec((None, tm, d), lambda b, i, j: (b, i, 0)),
        out_shape=jax.ShapeDtypeStruct((grp, r, d), F32),
        compiler_params=_cparams(("arbitrary", "arbitrary", "arbitrary")),
        name="out_proj_residual",
    )(o, w, x, gate)


def _mlp_body(x_ref, sh_ref, sc_ref, gt_ref, g_ref, w1_ref, w2_ref, y_ref, hn_ref, acc_ref, *, nf):
    f = pl.program_id(2)

    @pl.when(f == 0)
    def _():
        hn_ref[...] = _norm_mod(x_ref[...], g_ref[...], sc_ref[...], sh_ref[...]).astype(BF16)
        acc_ref[...] = jnp.zeros_like(acc_ref)

    h = jnp.maximum(_dot(hn_ref[...], w1_ref[...]), 0.0)
    acc_ref[...] += _dot((h * h).astype(BF16), w2_ref[...])

    @pl.when(f == nf - 1)
    def _():
        y_ref[...] = x_ref[...] + gt_ref[...] * acc_ref[...]


def mlp_residual(x, g, sh, sc, gate, w1, w2, *, tm, tf):
    grp, r, d = x.shape
    dff = w1.shape[1]
    tm = min(tm, r)
    nf = dff // tf
    return pl.pallas_call(
        functools.partial(_mlp_body, nf=nf),
        grid=(grp, r // tm, nf),
        in_specs=[
            pl.BlockSpec((None, tm, d), lambda b, i, f: (b, i, 0)),
            _mod_spec(sh.shape[1], tm, d),
            _mod_spec(sc.shape[1], tm, d),
            _mod_spec(gate.shape[1], tm, d),
            pl.BlockSpec((1, d), lambda b, i, f: (0, 0)),
            pl.BlockSpec((d, tf), lambda b, i, f: (0, f)),
            pl.BlockSpec((tf, d), lambda b, i, f: (f, 0)),
        ],
        out_specs=pl.BlockSpec((None, tm, d), lambda b, i, f: (b, i, 0)),
        out_shape=jax.ShapeDtypeStruct((grp, r, d), F32),
        scratch_shapes=[pltpu.VMEM((tm, d), BF16), pltpu.VMEM((tm, d), F32)],
        compiler_params=_cparams(("arbitrary", "arbitrary", "arbitrary")),
        name="mlp_residual",
    )(x, sh, sc, gate, g, w1, w2)


def _rms_lanes(x, gain, width):
    sq = x * x
    if width == LANES:
        ms = jnp.mean(sq, axis=-1, keepdims=True)
    else:
        low = lax.broadcasted_iota(I32, (1, LANES), 1) < width
        s_lo = jnp.sum(jnp.where(low, sq, 0.0), axis=-1, keepdims=True)
        s_hi = jnp.sum(jnp.where(low, 0.0, sq), axis=-1, keepdims=True)
        ms = jnp.where(low, s_lo, s_hi) * (1.0 / width)
    return x * lax.rsqrt(ms + NORM_EPS) * gain


def _rope128(x, cos, sin_s):
    return x * cos + pltpu.roll(x, 64, 1) * sin_s


def _rope64(x, cos, sin_s):
    first = (lax.broadcasted_iota(I32, (1, LANES), 1) % 64) < 32
    partner = jnp.where(first, pltpu.roll(x, 96, 1), pltpu.roll(x, 32, 1))
    return x * cos + partner * sin_s


VT_ROWS = HEAD_DIM + 16


def _store_vt(vt_ref, h, v):
    vt_ref[h, 0:HEAD_DIM] = v.T.astype(BF16)
    vt_ref[h, HEAD_DIM:VT_ROWS] = jnp.ones((VT_ROWS - HEAD_DIM, v.shape[0]), BF16)


def _post_a_body(raw_ref, idx_ref, c128_ref, s128_ref, c64_ref, s64_ref, qn_ref, kn_ref, ikn_ref,
                 q_ref, kf_ref, kb_ref, vt_ref, q3_ref, kif_ref, k3_ref, wi_ref):
    c128, s128 = c128_ref[...], s128_ref[...]
    c64, s64 = c64_ref[...], s64_ref[...]
    d = N_HEADS * HEAD_DIM
    for h in range(N_HEADS):
        sl = slice(h * HEAD_DIM, (h + 1) * HEAD_DIM)
        q = _rope128(_rms_lanes(raw_ref[:, sl], qn_ref[...], LANES), c128, s128)
        q_ref[:, sl] = (q * HEAD_DIM ** -0.5).astype(BF16)
        k = _rope128(_rms_lanes(raw_ref[:, d + h * HEAD_DIM:d + (h + 1) * HEAD_DIM], kn_ref[...], LANES), c128, s128)
        kf_ref[:, sl] = k
        kb_ref[:, sl] = k.astype(BF16)
        qi = _rope64(idx_ref[:, sl], c64, s64) * IDX_DIM ** -0.5
        qi_hi, qi_lo = _split_bf16(qi)
        q3_ref[h, :, 0:LANES] = (qi_hi.astype(F32) + pltpu.roll(qi_lo.astype(F32), 64, 1)).astype(BF16)
        q3_ref[h, :, LANES:2 * LANES] = qi_hi
        _store_vt(vt_ref, h, raw_ref[:, 2 * d + h * HEAD_DIM:2 * d + (h + 1) * HEAD_DIM])
    ki_raw = idx_ref[:, d:d + LANES]
    ms = jnp.sum(ki_raw * ki_raw, axis=-1, keepdims=True) * (1.0 / IDX_DIM)
    ki = _rope64(ki_raw * lax.rsqrt(ms + NORM_EPS) * ikn_ref[...], c64, s64)
    kif_ref[...] = ki[:, :IDX_DIM]
    ki_hi, ki_lo = _split_bf16(ki)
    k3_ref[:, 0:LANES] = (ki_hi.astype(F32) + pltpu.roll(ki_hi.astype(F32), 64, 1)).astype(BF16)
    k3_ref[:, LANES:2 * LANES] = ki_lo
    wi_ref[...] = idx_ref[:, d + LANES:d + LANES + IDX_HEADS] * IDX_HEADS ** -0.5


def post_a(raw, idx, tabs128, tabs64, qn, kn, ikn, *, tm):
    grp, r, _ = raw.shape
    tm = min(tm, r)
    d = N_HEADS * HEAD_DIM
    rt = tabs128[0].shape[0]
    tab_spec = (pl.BlockSpec((1, LANES), lambda b, i: (0, 0)) if rt == 1
                else pl.BlockSpec((tm, LANES), lambda b, i: (i, 0)))
    vec_spec = pl.BlockSpec((1, LANES), lambda b, i: (0, 0))
    row = lambda w: pl.BlockSpec((None, tm, w), lambda b, i: (b, i, 0))
    return pl.pallas_call(
        _post_a_body,
        grid=(grp, r // tm),
        in_specs=[row(3 * d), row(idx.shape[2]), tab_spec, tab_spec, tab_spec, tab_spec,
                  vec_spec, vec_spec, vec_spec],
        out_specs=[row(d), row(d), row(d),
                   pl.BlockSpec((None, N_HEADS, VT_ROWS, tm), lambda b, i: (b, 0, 0, i)),
                   pl.BlockSpec((None, IDX_HEADS, tm, 2 * LANES), lambda b, i: (b, 0, i, 0)),
                   row(IDX_DIM), row(2 * LANES), row(IDX_HEADS)],
        out_shape=[jax.ShapeDtypeStruct((grp, r, d), BF16), jax.ShapeDtypeStruct((grp, r, d), F32),
                   jax.ShapeDtypeStruct((grp, r, d), BF16), jax.ShapeDtypeStruct((grp, N_HEADS, VT_ROWS, r), BF16),
                   jax.ShapeDtypeStruct((grp, IDX_HEADS, r, 2 * LANES), BF16),
                   jax.ShapeDtypeStruct((grp, r, IDX_DIM), F32),
                   jax.ShapeDtypeStruct((grp, r, 2 * LANES), BF16),
                   jax.ShapeDtypeStruct((grp, r, IDX_HEADS), F32)],
        compiler_params=_cparams(("arbitrary", "arbitrary")),
        name="post_dsa",
    )(raw, idx, tabs128[0], tabs128[1], tabs64[0], tabs64[1], qn, kn, ikn)


def _post_b_body(raw_ref, c64_ref, s64_ref, qn_ref, kn_ref, q2_ref, kf_ref, kb_ref, vt_ref):
    c64, s64 = c64_ref[...], s64_ref[...]
    d = N_HEADS * HEAD_DIM
    low = lax.broadcasted_iota(I32, (1, LANES), 1) < 64
    for h in range(N_HEADS):
        sl = slice(h * HEAD_DIM, (h + 1) * HEAD_DIM)
        q = _rope64(_rms_lanes(raw_ref[:, sl], qn_ref[...], 64), c64, s64) * (HEAD_DIM // 2) ** -0.5
        q2_ref[0, :, sl] = jnp.where(low, q, 0.0).astype(BF16)
        q2_ref[1, :, sl] = jnp.where(low, 0.0, q).astype(BF16)
        k = _rope64(_rms_lanes(raw_ref[:, d + h * HEAD_DIM:d + (h + 1) * HEAD_DIM], kn_ref[...], 64), c64, s64)
        kf_ref[:, sl] = k
        kb_ref[:, sl] = k.astype(BF16)
        _store_vt(vt_ref, h, raw_ref[:, 2 * d + h * HEAD_DIM:2 * d + (h + 1) * HEAD_DIM])


def post_b(raw, tabs64, qn, kn, *, tm):
    grp, r, _ = raw.shape
    tm = min(tm, r)
    d = N_HEADS * HEAD_DIM
    rt = tabs64[0].shape[0]
    tab_spec = (pl.BlockSpec((1, LANES), lambda b, i: (0, 0)) if rt == 1
                else pl.BlockSpec((tm, LANES), lambda b, i: (i, 0)))
    vec_spec = pl.BlockSpec((1, LANES), lambda b, i: (0, 0))
    row = lambda w: pl.BlockSpec((None, tm, w), lambda b, i: (b, i, 0))
    return pl.pallas_call(
        _post_b_body,
        grid=(grp, r // tm),
        in_specs=[row(3 * d), tab_spec, tab_spec, vec_spec, vec_spec],
        out_specs=[pl.BlockSpec((None, 2, tm, d), lambda b, i: (b, 0, i, 0)), row(d), row(d),
                   pl.BlockSpec((None, N_HEADS, VT_ROWS, tm), lambda b, i: (b, 0, 0, i))],
        out_shape=[jax.ShapeDtypeStruct((grp, 2, r, d), BF16), jax.ShapeDtypeStruct((grp, r, d), F32),
                   jax.ShapeDtypeStruct((grp, r, d), BF16), jax.ShapeDtypeStruct((grp, N_HEADS, VT_ROWS, r), BF16)],
        compiler_params=_cparams(("arbitrary", "arbitrary")),
        name="post_diff",
    )(raw, tabs64[0], tabs64[1], qn, kn)


def _post_c_body(raw_ref, q_ref, kb_ref, vb_ref):
    d = N_HEADS * HEAD_DIM
    q_ref[...] = (raw_ref[:, 0:d] * HEAD_DIM ** -0.5).astype(BF16)
    kb_ref[...] = raw_ref[:, d:2 * d].astype(BF16)
    vb_ref[...] = raw_ref[:, 2 * d:3 * d].astype(BF16)


def post_c(raw, *, tm):
    grp, r, _ = raw.shape
    tm = min(tm, r)
    d = N_HEADS * HEAD_DIM
    row = lambda w: pl.BlockSpec((None, tm, w), lambda b, i: (b, i, 0))
    return pl.pallas_call(
        _post_c_body,
        grid=(grp, r // tm),
        in_specs=[row(3 * d)],
        out_specs=[row(d), row(d), row(d)],
        out_shape=[jax.ShapeDtypeStruct((grp, r, d), BF16)] * 3,
        compiler_params=_cparams(("arbitrary", "arbitrary")),
        name="post_sb",
    )(raw)


def _sort_key(score):
    bits = pltpu.bitcast(score, I32)
    bits = jnp.where(bits == INT_MIN, 0, bits)
    return jnp.where(bits < 0, bits ^ 0x7FFFFFFF, bits)


def _lane_total(cnt):
    tot = jnp.sum(cnt.astype(F32), axis=1, keepdims=True)
    return jnp.broadcast_to(tot, cnt.shape).astype(I32)


def _kth_largest(count_ge, rows, ktop, nbits=32):
    def step(i, cur):
        trial = cur + lax.shift_left(jnp.int32(1), nbits - 1 - i)
        return jnp.where(count_ge(trial) >= ktop, trial, cur)

    lowest = INT_MIN if nbits == 32 else -(1 << (nbits - 1))
    return lax.fori_loop(0, nbits, step, jnp.full((rows, LANES), lowest, I32))


def _tie_cutoff(count_eq_below, need, rows, nbits):
    def step(i, cur):
        trial = cur + lax.shift_left(jnp.int32(1), nbits - 1 - i)
        return jnp.where(count_eq_below(trial) < need, trial, cur)

    return lax.fori_loop(0, nbits, step, jnp.zeros((rows, LANES), I32))


def _dsa_select_body(q3_ref, wi_ref, k3_ref, bias_ref, key_ref, cut_ref, *, tq, ck, t_len, ktop):
    qb = pl.program_id(1)
    row0 = qb * tq
    nch = (row0 + tq - 1) // ck + 1
    nsub = ck // LANES
    q3 = q3_ref[...].reshape(IDX_HEADS * tq, 2 * LANES)
    wi = wi_ref[...]
    rows = row0 + lax.broadcasted_iota(I32, (tq, ck), 0)
    lane_cols = lax.broadcasted_iota(I32, (tq, ck), 1)

    def score_chunk(c, carry):
        c0 = pl.multiple_of(c * ck, ck)
        s = _dot_nt(q3, k3_ref[pl.ds(c0, ck), :])
        acc = jnp.zeros((tq, ck), F32)
        for h in range(IDX_HEADS):
            acc = acc + jnp.maximum(s[h * tq:(h + 1) * tq], 0.0) * wi[:, h:h + 1]
        key_ref[c] = jnp.where(c0 + lane_cols <= rows, _sort_key(acc), INT_MIN)
        return carry

    lax.fori_loop(0, nch, score_chunk, 0)

    def count(ind):
        def body(c, cnt):
            m = ind(key_ref[c], c * ck + lane_cols)
            for j in range(nsub):
                cnt = cnt + m[:, j * LANES:(j + 1) * LANES]
            return cnt
        return _lane_total(lax.fori_loop(0, nch, body, jnp.zeros((tq, LANES), I32)))

    def wide(x):
        return jnp.concatenate([x] * nsub, axis=1)

    thr = _kth_largest(lambda trial: count(lambda kc, cols: jnp.where(kc >= wide(trial), 1, 0)), tq, ktop)
    thr_w = wide(thr)
    need = ktop - count(lambda kc, cols: jnp.where(kc > thr_w, 1, 0))
    n_eq = count(lambda kc, cols: jnp.where(kc == thr_w, 1, 0))
    cut_ref[...] = jnp.full((tq, LANES), t_len, I32)

    @pl.when(jnp.max(n_eq - need) > 0)
    def _():
        nbits = max(1, (t_len - 1).bit_length())
        cut_ref[...] = _tie_cutoff(
            lambda trial: count(
                lambda kc, cols: jnp.where(kc == thr_w, jnp.where(cols < wide(trial), 1, 0), 0)),
            need, tq, nbits)

    cut_w = wide(cut_ref[...])

    def write_chunk(c, carry):
        kc = key_ref[c]
        cols = c * ck + lane_cols
        tie = jnp.where(kc == thr_w, jnp.where(cols <= cut_w, 0.0, NEG), NEG)
        sel = jnp.where(kc > thr_w, 0.0, tie)
        bias_ref[c] = jnp.where(cols <= rows, sel, NEG).astype(BF16)
        return carry

    lax.fori_loop(0, nch, write_chunk, 0)

    def fill_chunk(c, carry):
        bias_ref[c] = jnp.full((tq, ck), NEG, BF16)
        return carry

    lax.fori_loop(nch, t_len // ck, fill_chunk, 0)


def dsa_select(q3, wi, k3, *, tq, ck):
    b, _, t_len, _ = q3.shape
    ktop = min(TOPK_MAX, t_len // 4)
    nck = t_len // ck
    return pl.pallas_call(
        functools.partial(_dsa_select_body, tq=tq, ck=ck, t_len=t_len, ktop=ktop),
        grid=(b, t_len // tq),
        in_specs=[
            pl.BlockSpec((None, IDX_HEADS, tq, 2 * LANES), lambda bb, i: (bb, 0, i, 0)),
            pl.BlockSpec((None, tq, IDX_HEADS), lambda bb, i: (bb, i, 0)),
            pl.BlockSpec((None, t_len, 2 * LANES), lambda bb, i: (bb, 0, 0)),
        ],
        out_specs=pl.BlockSpec((None, nck, tq, ck), lambda bb, i: (bb, 0, i, 0)),
        out_shape=jax.ShapeDtypeStruct((b, nck, t_len, ck), BF16),
        scratch_shapes=[pltpu.VMEM((nck, tq, ck), I32), pltpu.VMEM((tq, LANES), I32)],
        compiler_params=_cparams(("arbitrary", "arbitrary")),
        name="dsa_select",
    )(q3, wi, k3)


def _kmax(qi, tq, bk):
    return ((qi + 1) * tq - 1) // bk


def _softmax_update_t(s_t, m_ref, acc_ref, v_t, idx):
    m_old = m_ref[idx]
    m_new = jnp.maximum(m_old, jnp.max(s_t, axis=0, keepdims=True))
    alpha = jnp.exp(m_old - m_new)
    p = jnp.exp(s_t - m_new)
    acc_ref[idx] = alpha * acc_ref[idx] + _dot(v_t, p.astype(BF16))
    m_ref[idx] = m_new


def _attn_init(m_ref, acc_ref):
    m_ref[...] = jnp.full_like(m_ref, NEG)
    acc_ref[...] = jnp.zeros_like(acc_ref)


def _dsa_attn_body(q_ref, k_ref, vt_ref, b_ref, o_ref, m_ref, acc_ref, *, tq, bk, nk):
    qi, ki = pl.program_id(1), pl.program_id(2)

    @pl.when(ki == 0)
    def _():
        _attn_init(m_ref, acc_ref)

    @pl.when(ki <= _kmax(qi, tq, bk))
    def _():
        bias_t = b_ref[...].astype(F32).T
        for h in range(N_HEADS):
            sl = slice(h * HEAD_DIM, (h + 1) * HEAD_DIM)
            s_t = _dot_nt(k_ref[:, sl], q_ref[:, sl]) + bias_t
            _softmax_update_t(s_t, m_ref, acc_ref, vt_ref[h], h)

    @pl.when(ki == nk - 1)
    def _():
        for h in range(N_HEADS):
            sl = slice(h * HEAD_DIM, (h + 1) * HEAD_DIM)
            o_ref[:, sl] = (acc_ref[h, 0:HEAD_DIM] / acc_ref[h, HEAD_DIM:HEAD_DIM + 1]).T.astype(BF16)


def _kv_specs(tq, bk, d):
    k_spec = pl.BlockSpec((None, bk, d), lambda bb, i, j: (bb, jnp.minimum(j, _kmax(i, tq, bk)), 0))
    vt_spec = pl.BlockSpec((None, N_HEADS, VT_ROWS, bk),
                           lambda bb, i, j: (bb, 0, 0, jnp.minimum(j, _kmax(i, tq, bk))))
    return k_spec, vt_spec


def dsa_attention(q, k, v_t, bias, *, tq):
    b, t_len, d = q.shape
    bk = bias.shape[3]
    nk = t_len // bk
    k_spec, vt_spec = _kv_specs(tq, bk, d)
    return pl.pallas_call(
        functools.partial(_dsa_attn_body, tq=tq, bk=bk, nk=nk),
        grid=(b, t_len // tq, nk),
        in_specs=[
            pl.BlockSpec((None, tq, d), lambda bb, i, j: (bb, i, 0)),
            k_spec, vt_spec,
            pl.BlockSpec((None, None, tq, bk), lambda bb, i, j: (bb, jnp.minimum(j, _kmax(i, tq, bk)), i, 0)),
        ],
        out_specs=pl.BlockSpec((None, tq, d), lambda bb, i, j: (bb, i, 0)),
        out_shape=jax.ShapeDtypeStruct((b, t_len, d), BF16),
        scratch_shapes=[pltpu.VMEM((N_HEADS, 1, tq), F32), pltpu.VMEM((N_HEADS, VT_ROWS, tq), F32)],
        compiler_params=_cparams(("arbitrary", "arbitrary", "arbitrary")),
        name="dsa_attention",
    )(q, k, v_t, bias)


def _diff_lambda(lam_ref, lam_init):
    lp = lam_ref[...]
    a = jnp.sum(lp[0:1] * lp[1:2], axis=-1, keepdims=True)
    b = jnp.sum(lp[2:3] * lp[3:4], axis=-1, keepdims=True)
    return jnp.exp(a) - jnp.exp(b) + lam_init


def _diff_attn_body(q2_ref, k_ref, vt_ref, lam_ref, sub_ref, o_ref, m_ref, acc_ref,
                    *, tq, bk, nk, lam_init):
    qi, ki = pl.program_id(1), pl.program_id(2)
    kmax = _kmax(qi, tq, bk)

    @pl.when(ki == 0)
    def _():
        _attn_init(m_ref, acc_ref)

    def step(masked):
        q2 = q2_ref[...].reshape(2 * tq, N_HEADS * HEAD_DIM)
        if masked:
            keys = ki * bk + lax.broadcasted_iota(I32, (bk, 2 * tq), 0)
            queries = qi * tq + lax.broadcasted_iota(I32, (bk, 2 * tq), 1) % tq
            allowed = keys <= queries
        for h in range(N_HEADS):
            sl = slice(h * HEAD_DIM, (h + 1) * HEAD_DIM)
            s_t = _dot_nt(k_ref[:, sl], q2[:, sl])
            if masked:
                s_t = jnp.where(allowed, s_t, NEG)
            _softmax_update_t(s_t, m_ref, acc_ref, vt_ref[h], h)

    @pl.when((ki + 1) * bk - 1 <= qi * tq)
    def _():
        step(False)

    @pl.when(jnp.logical_and((ki + 1) * bk - 1 > qi * tq, ki <= kmax))
    def _():
        step(True)

    @pl.when(ki == nk - 1)
    def _():
        lam = _diff_lambda(lam_ref, lam_init)
        for h in range(N_HEADS):
            sl = slice(h * HEAD_DIM, (h + 1) * HEAD_DIM)
            o_all = acc_ref[h, 0:HEAD_DIM] / acc_ref[h, HEAD_DIM:HEAD_DIM + 1]
            o = o_all[:, 0:tq] - lam * o_all[:, tq:2 * tq]
            ms = jnp.mean(o * o, axis=0, keepdims=True)
            o = o * lax.rsqrt(ms + NORM_EPS) * sub_ref[...] * (1.0 - lam_init)
            o_ref[:, sl] = o.T.astype(BF16)


def diff_attention(q2, k, v_t, lam_params, subln_col, lam_init, *, tq, bk):
    b, _, t_len, d = q2.shape
    nk = t_len // bk
    k_spec, vt_spec = _kv_specs(tq, bk, d)
    return pl.pallas_call(
        functools.partial(_diff_attn_body, tq=tq, bk=bk, nk=nk, lam_init=lam_init),
        grid=(b, t_len // tq, nk),
        in_specs=[
            pl.BlockSpec((None, 2, tq, d), lambda bb, i, j: (bb, 0, i, 0)),
            k_spec, vt_spec,
            pl.BlockSpec(lam_params.shape, lambda bb, i, j: (0, 0)),
            pl.BlockSpec((HEAD_DIM, 1), lambda bb, i, j: (0, 0)),
        ],
        out_specs=pl.BlockSpec((None, tq, d), lambda bb, i, j: (bb, i, 0)),
        out_shape=jax.ShapeDtypeStruct((b, t_len, d), BF16),
        scratch_shapes=[pltpu.VMEM((N_HEADS, 1, 2 * tq), F32), pltpu.VMEM((N_HEADS, VT_ROWS, 2 * tq), F32)],
        compiler_params=_cparams(("arbitrary", "arbitrary", "arbitrary")),
        name="diff_attention",
    )(q2, k, v_t, lam_params, subln_col)


def _softplus(z):
    return jnp.maximum(z, 0.0) + jnp.log1p(jnp.exp(-jnp.abs(z)))


def _sb_attn_body(q_ref, k_ref, v_ref, u_ref, o_ref, c_ref, acc_ref, done_ref, *, tq, bk, sub, nk):
    qi, ki = pl.program_id(1), pl.program_id(2)
    kb = _kmax(qi, tq, bk) - ki

    @pl.when(ki == 0)
    def _():
        c_ref[...] = jnp.zeros_like(c_ref)
        acc_ref[...] = jnp.zeros_like(acc_ref)
        done_ref[0] = 0

    @pl.when(jnp.logical_and(kb >= 0, done_ref[0] == 0))
    def _():
        rows = qi * tq + lax.broadcasted_iota(I32, (tq, sub), 0)
        lane_cols = lax.broadcasted_iota(I32, (tq, sub), 1)
        u = u_ref[...]
        for h in range(N_HEADS):
            sl = slice(h * HEAD_DIM, (h + 1) * HEAD_DIM)
            crun = c_ref[h]
            q = q_ref[:, sl]
            for j in reversed(range(bk // sub)):
                ks = slice(j * sub, (j + 1) * sub)
                z = _dot_nt(q, k_ref[ks, sl])
                mask = kb * bk + j * sub + lane_cols < rows
                sp = _softplus(z)
                lk = jnp.where(mask, -sp, 0.0)
                lk_hi, lk_lo = _split_bf16(lk)
                after = crun + _dot(lk_hi, u) + _dot(lk_lo, u)
                a = jnp.where(mask, jnp.exp(after + z - sp), 0.0)
                acc_ref[:, sl] += _dot(a.astype(BF16), v_ref[ks, sl])
                crun = crun + jnp.sum(lk, axis=-1, keepdims=True)
            c_ref[h] = crun
        done_ref[0] = jnp.where(jnp.max(c_ref[...]) < SB_EXP_UNDERFLOW, 1, 0)

    @pl.when(ki == nk - 1)
    def _():
        o_ref[...] = acc_ref[...].astype(BF16)


def _later_key_matrix(n):
    j = lax.broadcasted_iota(I32, (n, n), 0)
    s = lax.broadcasted_iota(I32, (n, n), 1)
    return (j > s).astype(BF16)


def sb_attention(q, k, v, *, tq, bk, sub):
    b, t_len, d = q.shape
    nk = t_len // bk
    kv_spec = pl.BlockSpec((None, bk, d), lambda bb, i, j: (bb, jnp.maximum(_kmax(i, tq, bk) - j, 0), 0))
    return pl.pallas_call(
        functools.partial(_sb_attn_body, tq=tq, bk=bk, sub=sub, nk=nk),
        grid=(b, t_len // tq, nk),
        in_specs=[
            pl.BlockSpec((None, tq, d), lambda bb, i, j: (bb, i, 0)),
            kv_spec, kv_spec,
            pl.BlockSpec((sub, sub), lambda bb, i, j: (0, 0)),
        ],
        out_specs=pl.BlockSpec((None, tq, d), lambda bb, i, j: (bb, i, 0)),
        out_shape=jax.ShapeDtypeStruct((b, t_len, d), BF16),
        scratch_shapes=[pltpu.VMEM((N_HEADS, tq, 1), F32), pltpu.VMEM((tq, d), F32), pltpu.SMEM((1,), I32)],
        compiler_params=_cparams(("arbitrary", "arbitrary", "arbitrary")),
        name="sb_attention",
    )(q, k, v, _later_key_matrix(sub))


def _sample_scores_body(pt_ref, qh_ref, ql_ref, wi_ref, kn_ref, *rest, n_pages):
    page_refs, out_ref = rest[:n_pages], rest[n_pages]
    qh, ql = qh_ref[...], ql_ref[...]
    w = wi_ref[...]
    for p in range(n_pages):
        k_hi, k_lo = _split_bf16(page_refs[p][...])
        s = _dot_nt(qh, k_hi) + _dot_nt(ql, k_hi) + _dot_nt(qh, k_lo)
        out_ref[p:p + 1, :] = jnp.sum(jnp.maximum(s, 0.0) * w, axis=0, keepdims=True)
    q = qh.astype(F32) + ql.astype(F32)
    s_new = jnp.sum(q * kn_ref[...], axis=-1, keepdims=True)
    i_new = jnp.sum(jnp.maximum(s_new, 0.0) * w, axis=0, keepdims=True)
    lane = lax.broadcasted_iota(I32, (1, LANES), 1)
    out_ref[n_pages:n_pages + 1, :] = jnp.where(lane == 0, i_new, -jnp.inf)


def sample_scores(page_table, qh, ql, wi_col, ki_new, pool_kidx, layer):
    db, n_pages = page_table.shape
    page_specs = [
        pl.BlockSpec((None, None, PAGE_SIZE, IDX_DIM),
                     functools.partial(lambda r, pt, p: (layer, pt[r, p], 0, 0), p=p))
        for p in range(n_pages)]
    grid_spec = pltpu.PrefetchScalarGridSpec(
        num_scalar_prefetch=1,
        grid=(db,),
        in_specs=[
            pl.BlockSpec((None, IDX_HEADS, IDX_DIM), lambda r, pt: (r, 0, 0)),
            pl.BlockSpec((None, IDX_HEADS, IDX_DIM), lambda r, pt: (r, 0, 0)),
            pl.BlockSpec((None, IDX_HEADS, 1), lambda r, pt: (r, 0, 0)),
            pl.BlockSpec((None, 1, IDX_DIM), lambda r, pt: (r, 0, 0)),
        ] + page_specs,
        out_specs=pl.BlockSpec((None, n_pages + 1, LANES), lambda r, pt: (r, 0, 0)),
    )
    return pl.pallas_call(
        functools.partial(_sample_scores_body, n_pages=n_pages),
        grid_spec=grid_spec,
        out_shape=jax.ShapeDtypeStruct((db, n_pages + 1, LANES), F32),
        compiler_params=_cparams(("arbitrary",)),
        name="sample_scores",
    )(page_table, qh, ql, wi_col, ki_new, *([pool_kidx] * n_pages))


def _sample_select_body(sc_ref, e_ref, bias_ref, *, db, n_chunks, ktop):
    lane = lax.broadcasted_iota(I32, (db, LANES), 1)
    keys = []
    for c in range(n_chunks):
        s = sc_ref[:, c * LANES:(c + 1) * LANES]
        keys.append(jnp.where(s == -jnp.inf, INT_MIN, _sort_key(s)))

    def count(ind):
        cnt = jnp.zeros((db, LANES), I32)
        for c in range(n_chunks):
            cnt = cnt + ind(keys[c], c * LANES + lane)
        return _lane_total(cnt)

    thr = _kth_largest(lambda trial: count(lambda kc, cols: jnp.where(kc >= trial, 1, 0)), db, ktop)
    need = ktop - count(lambda kc, cols: jnp.where(kc > thr, 1, 0))
    nbits = max(1, (n_chunks * LANES - 1).bit_length())
    cut = _tie_cutoff(
        lambda trial: count(lambda kc, cols: jnp.where(kc == thr, jnp.where(cols < trial, 1, 0), 0)),
        need, db, nbits)
    e = e_ref[...]
    for c in range(n_chunks):
        cols = c * LANES + lane
        tie = jnp.where(keys[c] == thr, jnp.where(cols <= cut, 1.0, 0.0), 0.0)
        sel = jnp.where(keys[c] > thr, 1.0, tie)
        sel = jnp.where(keys[c] == INT_MIN, 0.0, sel)
        wide = _dot(sel.astype(BF16), e)
        bias_ref[c] = jnp.where(wide > 0.5, 0.0, NEG)


def sample_select(scores2d, ktop):
    db, n = scores2d.shape
    n_chunks = n // LANES
    flat = N_HEADS * PAGE_SIZE
    t_of_lane = lax.broadcasted_iota(I32, (PAGE_SIZE, flat), 1) // N_HEADS
    e = (t_of_lane == lax.broadcasted_iota(I32, (PAGE_SIZE, flat), 0)).astype(BF16)
    return pl.pallas_call(
        functools.partial(_sample_select_body, db=db, n_chunks=n_chunks, ktop=ktop),
        grid=(1,),
        in_specs=[pl.BlockSpec((db, n), lambda i: (0, 0)), pl.BlockSpec((PAGE_SIZE, flat), lambda i: (0, 0))],
        out_specs=pl.BlockSpec((n_chunks, db, flat), lambda i: (0, 0, 0)),
        out_shape=jax.ShapeDtypeStruct((n_chunks, db, flat), F32),
        compiler_params=_cparams(("arbitrary",)),
        name="sample_select",
    )(scores2d, e)


def _suffix_sum_stride8(x):
    n = x.shape[1]
    lane = lax.broadcasted_iota(I32, x.shape, 1)
    sh = N_HEADS
    while sh < n:
        x = x + jnp.where(lane < n - sh, pltpu.roll(x, n - sh, 1), 0.0)
        sh *= 2
    return x


def _sample_attn_body(pt_ref, q_ref, kn_ref, vn_ref, *rest, mode, pps, n_steps, lam_init):
    k_refs, v_refs = rest[:pps], rest[pps:2 * pps]
    rest = rest[2 * pps:]
    if mode == "dsa":
        bias_ref, bnew_ref, o_ref, m_ref, l_ref, acc_ref = rest
    elif mode == "diff":
        lam_ref, sub_ref, o_ref, m_ref, l_ref, acc_ref = rest
    else:
        o_ref, m_ref, l_ref, acc_ref = rest
    s_id = pl.program_id(1)
    nrow = q_ref.shape[0]
    flat = N_HEADS * PAGE_SIZE
    q = q_ref[...]
    valid = (lax.broadcasted_iota(I32, (nrow, flat), 1) % N_HEADS
             == lax.broadcasted_iota(I32, (nrow, flat), 0) % N_HEADS)

    @pl.when(s_id == 0)
    def _():
        m_ref[...] = jnp.full_like(m_ref, 0.0 if mode == "sb" else NEG)
        l_ref[...] = jnp.zeros_like(l_ref)
        acc_ref[...] = jnp.zeros_like(acc_ref)

    def k2d(i):
        return k_refs[i][...].reshape(flat, HEAD_DIM).astype(BF16)

    def weighted_values(weights):
        out = None
        for i in range(pps):
            pv = _dot(weights[i].astype(BF16), v_refs[i][...].reshape(flat, HEAD_DIM).astype(BF16))
            out = pv if out is None else out + pv
        return out

    if mode == "sb":
        z = [_dot_nt(q, k2d(i)) for i in range(pps)]
        sp = [_softplus(z[i]) for i in range(pps)]
        lk = [jnp.where(valid, -sp[i], 0.0) for i in range(pps)]
        run = m_ref[...]
        a = [None] * pps
        for i in reversed(range(pps)):
            after = run + (_suffix_sum_stride8(lk[i]) - lk[i])
            a[i] = jnp.where(valid, jnp.exp(after + z[i] - sp[i]), 0.0)
            run = run + jnp.sum(lk[i], axis=-1, keepdims=True)
        acc_ref[...] += weighted_values(a)
        m_ref[...] = run
    else:
        s = []
        for i in range(pps):
            s_i = jnp.where(valid, _dot_nt(q, k2d(i)), NEG)
            if mode == "dsa":
                s_i = s_i + bias_ref[i:i + 1, :]
            s.append(s_i)
        m_old = m_ref[...]
        s_max = functools.reduce(jnp.maximum, s)
        m_new = jnp.maximum(m_old, jnp.max(s_max, axis=-1, keepdims=True))
        alpha = jnp.exp(m_old - m_new)
        p = [jnp.where(valid, jnp.exp(s[i] - m_new), 0.0) for i in range(pps)]
        p_sum = functools.reduce(jnp.add, p)
        l_ref[...] = alpha * l_ref[...] + jnp.sum(p_sum, axis=-1, keepdims=True)
        acc_ref[...] = alpha * acc_ref[...] + weighted_values(p)
        m_ref[...] = m_new

    @pl.when(s_id == n_steps - 1)
    def _():
        reps = nrow // N_HEADS
        k_new = jnp.concatenate([kn_ref[...]] * reps, axis=0)
        v_new = jnp.concatenate([vn_ref[...]] * reps, axis=0)
        if mode == "sb":
            o_ref[...] = acc_ref[...] + jnp.zeros((nrow, 1), F32) * v_new
            return
        s_new = jnp.sum(q.astype(F32) * k_new, axis=-1, keepdims=True)
        if mode == "dsa":
            s_new = s_new + bnew_ref[:, 0:1]
        m_old = m_ref[...]
        m_new = jnp.maximum(m_old, s_new)
        alpha = jnp.exp(m_old - m_new)
        p_new = jnp.exp(s_new - m_new)
        l_fin = alpha * l_ref[...] + p_new
        o = (alpha * acc_ref[...] + p_new * v_new) / l_fin
        if mode == "diff":
            lam = _diff_lambda(lam_ref, lam_init)
            o = o[0:N_HEADS] - lam * o[N_HEADS:2 * N_HEADS]
            ms = jnp.mean(o * o, axis=-1, keepdims=True)
            o = o * lax.rsqrt(ms + NORM_EPS) * sub_ref[...] * (1.0 - lam_init)
        o_ref[...] = o


def sample_attention(mode, page_table, q_rows, k_new, v_new, pool_k, pool_v, layer, *, pps,
                     bias_pages=None, bias_new=None, lam_params=None, subln=None, lam_init=0.0):
    db, n_pages = page_table.shape
    nrow = q_rows.shape[1]
    n_steps = n_pages // pps
    flat = N_HEADS * PAGE_SIZE

    def page_map(i):
        if mode == "sb":
            return lambda r, s, pt: (layer, pt[r, (n_steps - 1 - s) * pps + i], 0, 0, 0)
        return lambda r, s, pt: (layer, pt[r, s * pps + i], 0, 0, 0)

    page_specs = [pl.BlockSpec((None, None, PAGE_SIZE, N_HEADS, HEAD_DIM), page_map(i)) for i in range(pps)]
    in_specs = [
        pl.BlockSpec((None, nrow, HEAD_DIM), lambda r, s, pt: (r, 0, 0)),
        pl.BlockSpec((None, N_HEADS, HEAD_DIM), lambda r, s, pt: (r, 0, 0)),
        pl.BlockSpec((None, N_HEADS, HEAD_DIM), lambda r, s, pt: (r, 0, 0)),
    ] + page_specs + page_specs
    args = [q_rows, k_new, v_new] + [pool_k] * pps + [pool_v] * pps
    if mode == "dsa":
        in_specs += [pl.BlockSpec((None, None, pps, flat), lambda r, s, pt: (r, s, 0, 0)),
                     pl.BlockSpec((None, 1, flat), lambda r, s, pt: (r, 0, 0))]
        args += [bias_pages, bias_new]
    elif mode == "diff":
        in_specs += [pl.BlockSpec(lam_params.shape, lambda r, s, pt: (0, 0)),
                     pl.BlockSpec((1, HEAD_DIM), lambda r, s, pt: (0, 0))]
        args += [lam_params, subln]
    grid_spec = pltpu.PrefetchScalarGridSpec(
        num_scalar_prefetch=1,
        grid=(db, n_steps),
        in_specs=in_specs,
        out_specs=pl.BlockSpec((None, N_HEADS, HEAD_DIM), lambda r, s, pt: (r, 0, 0)),
        scratch_shapes=[pltpu.VMEM((nrow, 1), F32), pltpu.VMEM((nrow, 1), F32),
                        pltpu.VMEM((nrow, HEAD_DIM), F32)],
    )
    return pl.pallas_call(
        functools.partial(_sample_attn_body, mode=mode, pps=pps, n_steps=n_steps, lam_init=lam_init),
        grid_spec=grid_spec,
        out_shape=jax.ShapeDtypeStruct((db, N_HEADS, HEAD_DIM), F32),
        compiler_params=_cparams(("arbitrary", "arbitrary")),
        name="sample_attention_" + mode,
    )(page_table, *args)


def _idx_weight(w_in):
    d = w_in.shape[0]
    hd = N_HEADS * HEAD_DIM
    base = 3 * hd
    cols = []
    zeros64 = jnp.zeros((d, LANES - IDX_DIM), w_in.dtype)
    for h in range(IDX_HEADS):
        cols += [w_in[:, base + h * IDX_DIM:base + (h + 1) * IDX_DIM], zeros64]
    cols += [w_in[:, base + IDX_HEADS * IDX_DIM:base + IDX_HEADS * IDX_DIM + IDX_DIM], zeros64]
    wi = w_in[:, base + IDX_HEADS * IDX_DIM + IDX_DIM:]
    cols += [wi, jnp.zeros((d, LANES - wi.shape[1]), w_in.dtype)]
    return jnp.concatenate(cols, axis=1)


def kernel(x_prompt, x_sample, cache_a_k, cache_a_v, cache_a_kidx, cache_b_k, cache_b_v, cache_c_k, cache_c_v, page_table, c_prompt, c_sample, ada_w, ada_b, norm_mix, norm_ffn, ffn_w1, ffn_w2, a_w_in, a_q_norm, a_k_norm, a_idx_k_norm, a_w_out, b_w_in, b_q_norm, b_k_norm, b_lambda, b_subln, b_w_out, c_w_in, c_w_out):
    b, t_len, d = x_prompt.shape
    db = x_sample.shape[0]
    depth = ada_w.shape[0]
    n_pages = page_table.shape[1]
    past = n_pages * PAGE_SIZE
    hd = (N_HEADS, HEAD_DIM)

    n_c = b + db
    n_c_pad = -(-n_c // 8) * 8
    c_all = jnp.concatenate([c_prompt, c_sample, jnp.zeros((n_c_pad - n_c, d), F32)], axis=0)
    mod = ada_modulation_all(c_all, ada_w, ada_b)

    tabs128_p = rope_tables(0, t_len, 64)
    tabs64_p = rope_tables(0, t_len, 32)
    tabs128_s = tuple(x[0:1] for x in rope_tables(past, 8, 64))
    tabs64_s = tuple(x[0:1] for x in rope_tables(past, 8, 32))

    xp = x_prompt
    xs = x_sample.reshape(1, db, d)
    rows = {name: [] for name in ("a_k", "a_v", "a_kidx", "b_k", "b_v", "c_k", "c_v")}
    vec = lambda a: a.reshape(1, -1)

    for i in range(depth):
        kind, j = i % N_MIXERS, i // N_MIXERS
        mp = [mod[i, :b, k * d:(k + 1) * d].reshape(b, 1, d) for k in range(6)]
        ms = [mod[i, b:n_c, k * d:(k + 1) * d].reshape(1, db, d) for k in range(6)]
        g_mix, g_ffn = vec(norm_mix[i]), vec(norm_ffn[i])

        if kind == 0:
            w_in = a_w_in[j]
            w_qkv = (w_in[:, :3 * d].astype(BF16),)
            w_idx = _split_bf16(_idx_weight(w_in))
            qn, kn = vec(a_q_norm[j]), vec(a_k_norm[j])
            ikn = jnp.pad(a_idx_k_norm[j], (0, LANES - IDX_DIM)).reshape(1, LANES)
            w_out = a_w_out[j].astype(BF16)

            raw = proj(xp, g_mix, mp[0], mp[1], w_qkv, tm=512, tn=1024, name="proj_dsa")
            idx = proj(xp, g_mix, mp[0], mp[1], w_idx, tm=512, tn=w_idx[0].shape[1], name="proj_dsa_idx")
            q_bf, k_f, k_bf, v_t, q3, ki_f, k3, wi = post_a(raw, idx, tabs128_p, tabs64_p, qn, kn, ikn, tm=256)
            bias = dsa_select(q3, wi, k3, tq=256, ck=512)
            op = dsa_attention(q_bf, k_bf, v_t, bias, tq=512)
            kp, vp, kip = k_f.reshape(b, t_len, *hd), raw[:, :, 2 * d:].reshape(b, t_len, *hd), ki_f

            raw_s = proj(xs, g_mix, ms[0], ms[1], w_qkv, tm=128, tn=1024, name="proj_dsa")
            idx_s = proj(xs, g_mix, ms[0], ms[1], w_idx, tm=128, tn=w_idx[0].shape[1], name="proj_dsa_idx")
            qs_bf, ks_f, _, _, q3_s, kis_f, _, wi_s = post_a(raw_s, idx_s, tabs128_s, tabs64_s, qn, kn, ikn, tm=128)
            qh = jnp.transpose(q3_s[0, :, :, 0:IDX_DIM], (1, 0, 2))
            ql = jnp.transpose(q3_s[0, :, :, IDX_DIM:2 * IDX_DIM], (1, 0, 2))
            scores = sample_scores(page_table, qh, ql, wi_s.reshape(db, IDX_HEADS, 1),
                                   kis_f.reshape(db, 1, IDX_DIM), cache_a_kidx, j)
            ktop = min(TOPK_MAX, (past + 1) // 4)
            sel_bias = sample_select(scores.reshape(db, (n_pages + 1) * LANES), ktop)
            pps = 8
            bias_pages = jnp.transpose(sel_bias[:n_pages], (1, 0, 2)).reshape(db, n_pages // pps, pps, -1)
            bias_new = jnp.transpose(sel_bias[n_pages:], (1, 0, 2))
            ks_, vs_ = ks_f.reshape(db, *hd), raw_s[0, :, 2 * d:].reshape(db, *hd)
            os_ = sample_attention("dsa", page_table, qs_bf.reshape(db, *hd), ks_, vs_, cache_a_k, cache_a_v, j,
                                   pps=pps, bias_pages=bias_pages, bias_new=bias_new)
            rows["a_k"].append((kp, ks_))
            rows["a_v"].append((vp, vs_))
            rows["a_kidx"].append((kip, kis_f.reshape(db, 1, IDX_DIM)))
        elif kind == 1:
            lam_init = 0.8 - 0.6 * math.exp(-0.3 * i)
            w_in = (b_w_in[j].astype(BF16),)
            qn = vec(jnp.tile(b_q_norm[j], 2))
            kn = vec(jnp.tile(b_k_norm[j], 2))
            subln = vec(b_subln[j])
            w_out = b_w_out[j].astype(BF16)

            raw = proj(xp, g_mix, mp[0], mp[1], w_in, tm=512, tn=1024, name="proj_diff")
            q2, k_f, k_bf, v_t = post_b(raw, tabs64_p, qn, kn, tm=256)
            op = diff_attention(q2, k_bf, v_t, b_lambda[j], subln.reshape(HEAD_DIM, 1), lam_init, tq=512, bk=512)
            kp, vp = k_f.reshape(b, t_len, *hd), raw[:, :, 2 * d:].reshape(b, t_len, *hd)

            raw_s = proj(xs, g_mix, ms[0], ms[1], w_in, tm=128, tn=1024, name="proj_diff")
            q2_s, ks_f, _, _ = post_b(raw_s, tabs64_s, qn, kn, tm=128)
            q_rows = jnp.transpose(q2_s[0].reshape(2, db, *hd), (1, 0, 2, 3)).reshape(db, 2 * N_HEADS, HEAD_DIM)
            ks_, vs_ = ks_f.reshape(db, *hd), raw_s[0, :, 2 * d:].reshape(db, *hd)
            os_ = sample_attention("diff", page_table, q_rows, ks_, vs_, cache_b_k, cache_b_v, j, pps=8,
                                   lam_params=b_lambda[j], subln=subln, lam_init=lam_init)
            rows["b_k"].append((kp, ks_))
            rows["b_v"].append((vp, vs_))
        else:
            w_in = (c_w_in[j].astype(BF16),)
            w_out = c_w_out[j].astype(BF16)

            raw = proj(xp, g_mix, mp[0], mp[1], w_in, tm=512, tn=1024, name="proj_sb")
            q_bf, k_bf, v_bf = post_c(raw, tm=256)
            op = sb_attention(q_bf, k_bf, v_bf, tq=512, bk=512, sub=256)
            kp, vp = raw[:, :, d:2 * d].reshape(b, t_len, *hd), raw[:, :, 2 * d:].reshape(b, t_len, *hd)

            raw_s = proj(xs, g_mix, ms[0], ms[1], w_in, tm=128, tn=1024, name="proj_sb")
            qs_bf, _, _ = post_c(raw_s, tm=128)
            ks_, vs_ = raw_s[0, :, d:2 * d].reshape(db, *hd), raw_s[0, :, 2 * d:].reshape(db, *hd)
            os_ = sample_attention("sb", page_table, qs_bf.reshape(db, *hd), ks_, vs_, cache_c_k, cache_c_v, j, pps=8)
            rows["c_k"].append((kp, ks_))
            rows["c_v"].append((vp, vs_))

        w1, w2 = ffn_w1[i].astype(BF16), ffn_w2[i].astype(BF16)
        xp = out_proj_residual(op, w_out, xp, mp[2], tm=512)
        xp = mlp_residual(xp, g_ffn, mp[3], mp[4], mp[5], w1, w2, tm=1024, tf=1024)
        xs = out_proj_residual(os_.reshape(1, db, d).astype(BF16), w_out, xs, ms[2], tm=128)
        xs = mlp_residual(xs, g_ffn, ms[3], ms[4], ms[5], w1, w2, tm=128, tf=1024)

    def stack(name, which):
        return jnp.stack([r[which] for r in rows[name]])

    outs = [xp, xs.reshape(db, 1, d)]
    for which in (0, 1):
        for name in ("a_k", "a_v", "a_kidx", "b_k", "b_v", "c_k", "c_v"):
            y = stack(name, which)
            if which == 1 and name != "a_kidx":
                y = y.reshape(y.shape[0], db, 1, *hd)
            outs.append(y)
    return tuple(outs)
```

```python
import functools
import math

import jax
import jax.numpy as jnp
from jax import lax
from jax.experimental import pallas as pl
from jax.experimental.pallas import tpu as pltpu

F32 = jnp.float32
BF16 = jnp.bfloat16
I32 = jnp.int32

D_MODEL = 1024
N_HEADS = 8
HEAD_DIM = 128
IDX_HEADS = 8
IDX_DIM = 64
TOPK_MAX = 256
PAGE_SIZE = 128
ROPE_THETA = 10000.0
NORM_EPS = 1e-6
N_MIXERS = 3

LANES = 128
NEG = -1e30
SB_EXP_UNDERFLOW = -110.0
INT_MIN = -2147483648
VMEM_LIMIT = 56 * 1024 * 1024
NT_DIMS = (((1,), (1,)), ((), ()))


def _cparams(sem):
    return pltpu.CompilerParams(dimension_semantics=sem, vmem_limit_bytes=VMEM_LIMIT)


def _split_bf16(x):
    hi = x.astype(BF16)
    lo = (x - hi.astype(F32)).astype(BF16)
    return hi, lo


def _dot(a, b):
    return jnp.dot(a, b, preferred_element_type=F32)


def _dot_nt(a, b):
    return lax.dot_general(a, b, NT_DIMS, preferred_element_type=F32)


def _norm_mod(x, g, sc, sh):
    ms = jnp.mean(x * x, axis=-1, keepdims=True)
    return (x * lax.rsqrt(ms + NORM_EPS) * g) * (1.0 + sc) + sh


def _mod_spec(rm, tm, d):
    if rm == 1:
        return pl.BlockSpec((None, 1, d), lambda g, i, j: (g, 0, 0))
    return pl.BlockSpec((None, tm, d), lambda g, i, j: (g, i, 0))


def _ada_body(c_ref, w_ref, b_ref, o_ref):
    c = c_ref[...]
    s = c * jax.nn.sigmoid(c)
    s_hi, s_lo = _split_bf16(s)
    w_hi, w_lo = _split_bf16(w_ref[...])
    o_ref[...] = _dot(s_hi, w_hi) + _dot(s_lo, w_hi) + _dot(s_hi, w_lo) + b_ref[...]


def ada_modulation_all(c_all, ada_w, ada_b):
    depth, d, n = ada_w.shape
    r = c_all.shape[0]
    tn = 1536
    return pl.pallas_call(
        _ada_body,
        grid=(depth, n // tn),
        in_specs=[
            pl.BlockSpec((r, d), lambda i, j: (0, 0)),
            pl.BlockSpec((None, d, tn), lambda i, j: (i, 0, j)),
            pl.BlockSpec((None, 1, tn), lambda i, j: (i, 0, j)),
        ],
        out_specs=pl.BlockSpec((None, r, tn), lambda i, j: (i, 0, j)),
        out_shape=jax.ShapeDtypeStruct((depth, r, n), F32),
        compiler_params=_cparams(("arbitrary", "arbitrary")),
        name="ada_modulation",
    )(c_all, ada_w, ada_b.reshape(depth, 1, n))


def _rope_tab_body(invf_ref, sgn_ref, cos_ref, sin_ref, *, start, tr):
    i = pl.program_id(0)
    pos = (start + i * tr + lax.broadcasted_iota(I32, (tr, LANES), 0)).astype(F32)
    ang = pos * invf_ref[...]
    cos_ref[...] = jnp.cos(ang)
    sin_ref[...] = jnp.sin(ang) * sgn_ref[...]


def rope_tables(start, rows, half):
    inv_freq = ROPE_THETA ** (-jnp.arange(half, dtype=F32) / half)
    reps = LANES // (2 * half)
    invf = jnp.tile(jnp.concatenate([inv_freq, inv_freq]), reps).reshape(1, LANES)
    sgn = jnp.tile(jnp.concatenate([-jnp.ones((half,), F32), jnp.ones((half,), F32)]), reps).reshape(1, LANES)
    tr = min(rows, 512)
    return pl.pallas_call(
        functools.partial(_rope_tab_body, start=start, tr=tr),
        grid=(rows // tr,),
        in_specs=[pl.BlockSpec((1, LANES), lambda i: (0, 0)), pl.BlockSpec((1, LANES), lambda i: (0, 0))],
        out_specs=[pl.BlockSpec((tr, LANES), lambda i: (i, 0)), pl.BlockSpec((tr, LANES), lambda i: (i, 0))],
        out_shape=[jax.ShapeDtypeStruct((rows, LANES), F32)] * 2,
        compiler_params=_cparams(("arbitrary",)),
        name="rope_tables",
    )(invf, sgn)


def _proj_body(x_ref, sh_ref, sc_ref, g_ref, w_ref, o_ref, hn_ref):
    @pl.when(pl.program_id(2) == 0)
    def _():
        hn_ref[...] = _norm_mod(x_ref[...], g_ref[...], sc_ref[...], sh_ref[...]).astype(BF16)

    o_ref[...] = _dot(hn_ref[...], w_ref[...])


def _proj3_body(x_ref, sh_ref, sc_ref, g_ref, whi_ref, wlo_ref, o_ref, hhi_ref, hlo_ref):
    @pl.when(pl.program_id(2) == 0)
    def _():
        hi, lo = _split_bf16(_norm_mod(x_ref[...], g_ref[...], sc_ref[...], sh_ref[...]))
        hhi_ref[...] = hi
        hlo_ref[...] = lo

    o_ref[...] = (_dot(hhi_ref[...], whi_ref[...]) + _dot(hlo_ref[...], whi_ref[...])
                  + _dot(hhi_ref[...], wlo_ref[...]))


def proj(x, g, sh, sc, ws, *, tm, tn, name):
    grp, r, d = x.shape
    n = ws[0].shape[1]
    tm = min(tm, r)
    precise = len(ws) == 2
    w_spec = pl.BlockSpec((d, tn), lambda b, i, j: (0, j))
    return pl.pallas_call(
        _proj3_body if precise else _proj_body,
        grid=(grp, r // tm, n // tn),
        in_specs=[
            pl.BlockSpec((None, tm, d), lambda b, i, j: (b, i, 0)),
            _mod_spec(sh.shape[1], tm, d),
            _mod_spec(sc.shape[1], tm, d),
            pl.BlockSpec((1, d), lambda b, i, j: (0, 0)),
        ] + [w_spec] * len(ws),
        out_specs=pl.BlockSpec((None, tm, tn), lambda b, i, j: (b, i, j)),
        out_shape=jax.ShapeDtypeStruct((grp, r, n), F32),
        scratch_shapes=[pltpu.VMEM((tm, d), BF16)] * len(ws),
        compiler_params=_cparams(("arbitrary", "arbitrary", "arbitrary")),
        name=name,
    )(x, sh, sc, g, *ws)


def _oproj_body(o_ref, w_ref, x_ref, gt_ref, y_ref):
    y_ref[...] = x_ref[...] + gt_ref[...] * _dot(o_ref[...], w_ref[...])


def out_proj_residual(o, w, x, gate, *, tm):
    grp, r, d = x.shape
    tm = min(tm, r)
    return pl.pallas_call(
        _oproj_body,
        grid=(grp, r // tm, 1),
        in_specs=[
            pl.BlockSpec((None, tm, d), lambda b, i, j: (b, i, 0)),
            pl.BlockSpec((d, d), lambda b, i, j: (0, 0)),
            pl.BlockSpec((None, tm, d), lambda b, i, j: (b, i, 0)),
            _mod_spec(gate.shape[1], tm, d),
        ],
        out_specs=pl.BlockSpec((None, tm, d), lambda b, i, j: (b, i, 0)),
        out_shape=jax.ShapeDtypeStruct((grp, r, d), F32),
        compiler_params=_cparams(("arbitrary", "arbitrary", "arbitrary")),
        name="out_proj_residual",
    )(o, w, x, gate)


def _mlp_body(x_ref, sh_ref, sc_ref, gt_ref, g_ref, w1_ref, w2_ref, y_ref, hn_ref, acc_ref, *, nf):
    f = pl.program_id(2)

    @pl.when(f == 0)
    def _():
        hn_ref[...] = _norm_mod(x_ref[...], g_ref[...], sc_ref[...], sh_ref[...]).astype(BF16)
        acc_ref[...] = jnp.zeros_like(acc_ref)

    h = jnp.maximum(_dot(hn_ref[...], w1_ref[...]), 0.0)
    acc_ref[...] += _dot((h * h).astype(BF16), w2_ref[...])

    @pl.when(f == nf - 1)
    def _():
        y_ref[...] = x_ref[...] + gt_ref[...] * acc_ref[...]


def mlp_residual(x, g, sh, sc, gate, w1, w2, *, tm, tf):
    grp, r, d = x.shape
    dff = w1.shape[1]
    tm = min(tm, r)
    nf = dff // tf
    return pl.pallas_call(
        functools.partial(_mlp_body, nf=nf),
        grid=(grp, r // tm, nf),
        in_specs=[
            pl.BlockSpec((None, tm, d), lambda b, i, f: (b, i, 0)),
            _mod_spec(sh.shape[1], tm, d),
            _mod_spec(sc.shape[1], tm, d),
            _mod_spec(gate.shape[1], tm, d),
            pl.BlockSpec((1, d), lambda b, i, f: (0, 0)),
            pl.BlockSpec((d, tf), lambda b, i, f: (0, f)),
            pl.BlockSpec((tf, d), lambda b, i, f: (f, 0)),
        ],
        out_specs=pl.BlockSpec((None, tm, d), lambda b, i, f: (b, i, 0)),
        out_shape=jax.ShapeDtypeStruct((grp, r, d), F32),
        scratch_shapes=[pltpu.VMEM((tm, d), BF16), pltpu.VMEM((tm, d), F32)],
        compiler_params=_cparams(("arbitrary", "arbitrary", "arbitrary")),
        name="mlp_residual",
    )(x, sh, sc, gate, g, w1, w2)


def _rms_lanes(x, gain, width):
    sq = x * x
    if width == LANES:
        ms = jnp.mean(sq, axis=-1, keepdims=True)
    else:
        low = lax.broadcasted_iota(I32, (1, LANES), 1) < width
        s_lo = jnp.sum(jnp.where(low, sq, 0.0), axis=-1, keepdims=True)
        s_hi = jnp.sum(jnp.where(low, 0.0, sq), axis=-1, keepdims=True)
        ms = jnp.where(low, s_lo, s_hi) * (1.0 / width)
    return x * lax.rsqrt(ms + NORM_EPS) * gain


def _rope128(x, cos, sin_s):
    return x * cos + pltpu.roll(x, 64, 1) * sin_s


def _rope64(x, cos, sin_s):
    first = (lax.broadcasted_iota(I32, (1, LANES), 1) % 64) < 32
    partner = jnp.where(first, pltpu.roll(x, 96, 1), pltpu.roll(x, 32, 1))
    return x * cos + partner * sin_s


VT_ROWS = HEAD_DIM + 16


def _store_vt(vt_ref, h, v):
    vt_ref[h, 0:HEAD_DIM] = v.T.astype(BF16)
    vt_ref[h, HEAD_DIM:VT_ROWS] = jnp.ones((VT_ROWS - HEAD_DIM, v.shape[0]), BF16)


def _post_a_body(raw_ref, idx_ref, c128_ref, s128_ref, c64_ref, s64_ref, qn_ref, kn_ref, ikn_ref,
                 q_ref, kf_ref, kb_ref, vt_ref, q3_ref, kif_ref, k3_ref, wi_ref):
    c128, s128 = c128_ref[...], s128_ref[...]
    c64, s64 = c64_ref[...], s64_ref[...]
    d = N_HEADS * HEAD_DIM
    for h in range(N_HEADS):
        sl = slice(h * HEAD_DIM, (h + 1) * HEAD_DIM)
        q = _rope128(_rms_lanes(raw_ref[:, sl], qn_ref[...], LANES), c128, s128)
        q_ref[:, sl] = (q * HEAD_DIM ** -0.5).astype(BF16)
        k = _rope128(_rms_lanes(raw_ref[:, d + h * HEAD_DIM:d + (h + 1) * HEAD_DIM], kn_ref[...], LANES), c128, s128)
        kf_ref[:, sl] = k
        kb_ref[:, sl] = k.astype(BF16)
        qi = _rope64(idx_ref[:, sl], c64, s64) * IDX_DIM ** -0.5
        qi_hi, qi_lo = _split_bf16(qi)
        q3_ref[h, :, 0:LANES] = (qi_hi.astype(F32) + pltpu.roll(qi_lo.astype(F32), 64, 1)).astype(BF16)
        q3_ref[h, :, LANES:2 * LANES] = qi_hi
        _store_vt(vt_ref, h, raw_ref[:, 2 * d + h * HEAD_DIM:2 * d + (h + 1) * HEAD_DIM])
    ki_raw = idx_ref[:, d:d + LANES]
    ms = jnp.sum(ki_raw * ki_raw, axis=-1, keepdims=True) * (1.0 / IDX_DIM)
    ki = _rope64(ki_raw * lax.rsqrt(ms + NORM_EPS) * ikn_ref[...], c64, s64)
    kif_ref[...] = ki[:, :IDX_DIM]
    ki_hi, ki_lo = _split_bf16(ki)
    k3_ref[:, 0:LANES] = (ki_hi.astype(F32) + pltpu.roll(ki_hi.astype(F32), 64, 1)).astype(BF16)
    k3_ref[:, LANES:2 * LANES] = ki_lo
    wi_ref[...] = idx_ref[:, d + LANES:d + LANES + IDX_HEADS] * IDX_HEADS ** -0.5


def post_a(raw, idx, tabs128, tabs64, qn, kn, ikn, *, tm):
    grp, r, _ = raw.shape
    tm = min(tm, r)
    d = N_HEADS * HEAD_DIM
    rt = tabs128[0].shape[0]
    tab_spec = (pl.BlockSpec((1, LANES), lambda b, i: (0, 0)) if rt == 1
                else pl.BlockSpec((tm, LANES), lambda b, i: (i, 0)))
    vec_spec = pl.BlockSpec((1, LANES), lambda b, i: (0, 0))
    row = lambda w: pl.BlockSpec((None, tm, w), lambda b, i: (b, i, 0))
    return pl.pallas_call(
        _post_a_body,
        grid=(grp, r // tm),
        in_specs=[row(3 * d), row(idx.shape[2]), tab_spec, tab_spec, tab_spec, tab_spec,
                  vec_spec, vec_spec, vec_spec],
        out_specs=[row(d), row(d), row(d),
                   pl.BlockSpec((None, N_HEADS, VT_ROWS, tm), lambda b, i: (b, 0, 0, i)),
                   pl.BlockSpec((None, IDX_HEADS, tm, 2 * LANES), lambda b, i: (b, 0, i, 0)),
                   row(IDX_DIM), row(2 * LANES), row(IDX_HEADS)],
        out_shape=[jax.ShapeDtypeStruct((grp, r, d), BF16), jax.ShapeDtypeStruct((grp, r, d), F32),
                   jax.ShapeDtypeStruct((grp, r, d), BF16), jax.ShapeDtypeStruct((grp, N_HEADS, VT_ROWS, r), BF16),
                   jax.ShapeDtypeStruct((grp, IDX_HEADS, r, 2 * LANES), BF16),
                   jax.ShapeDtypeStruct((grp, r, IDX_DIM), F32),
                   jax.ShapeDtypeStruct((grp, r, 2 * LANES), BF16),
                   jax.ShapeDtypeStruct((grp, r, IDX_HEADS), F32)],
        compiler_params=_cparams(("arbitrary", "arbitrary")),
        name="post_dsa",
    )(raw, idx, tabs128[0], tabs128[1], tabs64[0], tabs64[1], qn, kn, ikn)


def _post_b_body(raw_ref, c64_ref, s64_ref, qn_ref, kn_ref, q2_ref, kf_ref, kb_ref, vt_ref):
    c64, s64 = c64_ref[...], s64_ref[...]
    d = N_HEADS * HEAD_DIM
    low = lax.broadcasted_iota(I32, (1, LANES), 1) < 64
    for h in range(N_HEADS):
        sl = slice(h * HEAD_DIM, (h + 1) * HEAD_DIM)
        q = _rope64(_rms_lanes(raw_ref[:, sl], qn_ref[...], 64), c64, s64) * (HEAD_DIM // 2) ** -0.5
        q2_ref[0, :, sl] = jnp.where(low, q, 0.0).astype(BF16)
        q2_ref[1, :, sl] = jnp.where(low, 0.0, q).astype(BF16)
        k = _rope64(_rms_lanes(raw_ref[:, d + h * HEAD_DIM:d + (h + 1) * HEAD_DIM], kn_ref[...], 64), c64, s64)
        kf_ref[:, sl] = k
        kb_ref[:, sl] = k.astype(BF16)
        _store_vt(vt_ref, h, raw_ref[:, 2 * d + h * HEAD_DIM:2 * d + (h + 1) * HEAD_DIM])


def post_b(raw, tabs64, qn, kn, *, tm):
    grp, r, _ = raw.shape
    tm = min(tm, r)
    d = N_HEADS * HEAD_DIM
    rt = tabs64[0].shape[0]
    tab_spec = (pl.BlockSpec((1, LANES), lambda b, i: (0, 0)) if rt == 1
                else pl.BlockSpec((tm, LANES), lambda b, i: (i, 0)))
    vec_spec = pl.BlockSpec((1, LANES), lambda b, i: (0, 0))
    row = lambda w: pl.BlockSpec((None, tm, w), lambda b, i: (b, i, 0))
    return pl.pallas_call(
        _post_b_body,
        grid=(grp, r // tm),
        in_specs=[row(3 * d), tab_spec, tab_spec, vec_spec, vec_spec],
        out_specs=[pl.BlockSpec((None, 2, tm, d), lambda b, i: (b, 0, i, 0)), row(d), row(d),
                   pl.BlockSpec((None, N_HEADS, VT_ROWS, tm), lambda b, i: (b, 0, 0, i))],
        out_shape=[jax.ShapeDtypeStruct((grp, 2, r, d), BF16), jax.ShapeDtypeStruct((grp, r, d), F32),
                   jax.ShapeDtypeStruct((grp, r, d), BF16), jax.ShapeDtypeStruct((grp, N_HEADS, VT_ROWS, r), BF16)],
        compiler_params=_cparams(("arbitrary", "arbitrary")),
        name="post_diff",
    )(raw, tabs64[0], tabs64[1], qn, kn)


def _post_c_body(raw_ref, q_ref, kb_ref, vb_ref):
    d = N_HEADS * HEAD_DIM
    q_ref[...] = (raw_ref[:, 0:d] * HEAD_DIM ** -0.5).astype(BF16)
    kb_ref[...] = raw_ref[:, d:2 * d].astype(BF16)
    vb_ref[...] = raw_ref[:, 2 * d:3 * d].astype(BF16)


def post_c(raw, *, tm):
    grp, r, _ = raw.shape
    tm = min(tm, r)
    d = N_HEADS * HEAD_DIM
    row = lambda w: pl.BlockSpec((None, tm, w), lambda b, i: (b, i, 0))
    return pl.pallas_call(
        _post_c_body,
        grid=(grp, r // tm),
        in_specs=[row(3 * d)],
        out_specs=[row(d), row(d), row(d)],
        out_shape=[jax.ShapeDtypeStruct((grp, r, d), BF16)] * 3,
        compiler_params=_cparams(("arbitrary", "arbitrary")),
        name="post_sb",
    )(raw)


def _sort_key(score):
    bits = pltpu.bitcast(score, I32)
    bits = jnp.where(bits == INT_MIN, 0, bits)
    return jnp.where(bits < 0, bits ^ 0x7FFFFFFF, bits)


def _lane_total(cnt):
    tot = jnp.sum(cnt.astype(F32), axis=1, keepdims=True)
    return jnp.broadcast_to(tot, cnt.shape).astype(I32)


def _kth_largest(count_ge, rows, ktop, nbits=32):
    def step(i, cur):
        trial = cur + lax.shift_left(jnp.int32(1), nbits - 1 - i)
        return jnp.where(count_ge(trial) >= ktop, trial, cur)

    lowest = INT_MIN if nbits == 32 else -(1 << (nbits - 1))
    return lax.fori_loop(0, nbits, step, jnp.full((rows, LANES), lowest, I32))


def _tie_cutoff(count_eq_below, need, rows, nbits):
    def step(i, cur):
        trial = cur + lax.shift_left(jnp.int32(1), nbits - 1 - i)
        return jnp.where(count_eq_below(trial) < need, trial, cur)

    return lax.fori_loop(0, nbits, step, jnp.zeros((rows, LANES), I32))


def _dsa_select_body(q3_ref, wi_ref, k3_ref, bias_ref, key_ref, cut_ref, *, tq, ck, t_len, ktop):
    qb = pl.program_id(1)
    row0 = qb * tq
    nch = (row0 + tq - 1) // ck + 1
    nsub = ck // LANES
    q3 = q3_ref[...].reshape(IDX_HEADS * tq, 2 * LANES)
    wi = wi_ref[...]
    rows = row0 + lax.broadcasted_iota(I32, (tq, ck), 0)
    lane_cols = lax.broadcasted_iota(I32, (tq, ck), 1)

    def score_chunk(c, carry):
        c0 = pl.multiple_of(c * ck, ck)
        s = _dot_nt(q3, k3_ref[pl.ds(c0, ck), :])
        acc = jnp.zeros((tq, ck), F32)
        for h in range(IDX_HEADS):
            acc = acc + jnp.maximum(s[h * tq:(h + 1) * tq], 0.0) * wi[:, h:h + 1]
        key_ref[c] = jnp.where(c0 + lane_cols <= rows, _sort_key(acc), INT_MIN)
        return carry

    lax.fori_loop(0, nch, score_chunk, 0)

    def count(ind):
        def body(c, cnt):
            m = ind(key_ref[c], c * ck + lane_cols)
            for j in range(nsub):
                cnt = cnt + m[:, j * LANES:(j + 1) * LANES]
            return cnt
        return _lane_total(lax.fori_loop(0, nch, body, jnp.zeros((tq, LANES), I32)))

    def wide(x):
        return jnp.concatenate([x] * nsub, axis=1)

    thr = _kth_largest(lambda trial: count(lambda kc, cols: jnp.where(kc >= wide(trial), 1, 0)), tq, ktop)
    thr_w = wide(thr)
    need = ktop - count(lambda kc, cols: jnp.where(kc > thr_w, 1, 0))
    n_eq = count(lambda kc, cols: jnp.where(kc == thr_w, 1, 0))
    cut_ref[...] = jnp.full((tq, LANES), t_len, I32)

    @pl.when(jnp.max(n_eq - need) > 0)
    def _():
        nbits = max(1, (t_len - 1).bit_length())
        cut_ref[...] = _tie_cutoff(
            lambda trial: count(
                lambda kc, cols: jnp.where(kc == thr_w, jnp.where(cols < wide(trial), 1, 0), 0)),
            need, tq, nbits)

    cut_w = wide(cut_ref[...])

    def write_chunk(c, carry):
        kc = key_ref[c]
        cols = c * ck + lane_cols
        tie = jnp.where(kc == thr_w, jnp.where(cols <= cut_w, 0.0, NEG), NEG)
        sel = jnp.where(kc > thr_w, 0.0, tie)
        bias_ref[c] = jnp.where(cols <= rows, sel, NEG).astype(BF16)
        return carry

    lax.fori_loop(0, nch, write_chunk, 0)

    def fill_chunk(c, carry):
        bias_ref[c] = jnp.full((tq, ck), NEG, BF16)
        return carry

    lax.fori_loop(nch, t_len // ck, fill_chunk, 0)


def dsa_select(q3, wi, k3, *, tq, ck):
    b, _, t_len, _ = q3.shape
    ktop = min(TOPK_MAX, t_len // 4)
    nck = t_len // ck
    return pl.pallas_call(
        functools.partial(_dsa_select_body, tq=tq, ck=ck, t_len=t_len, ktop=ktop),
        grid=(b, t_len // tq),
        in_specs=[
            pl.BlockSpec((None, IDX_HEADS, tq, 2 * LANES), lambda bb, i: (bb, 0, i, 0)),
            pl.BlockSpec((None, tq, IDX_HEADS), lambda bb, i: (bb, i, 0)),
            pl.BlockSpec((None, t_len, 2 * LANES), lambda bb, i: (bb, 0, 0)),
        ],
        out_specs=pl.BlockSpec((None, nck, tq, ck), lambda bb, i: (bb, 0, i, 0)),
        out_shape=jax.ShapeDtypeStruct((b, nck, t_len, ck), BF16),
        scratch_shapes=[pltpu.VMEM((nck, tq, ck), I32), pltpu.VMEM((tq, LANES), I32)],
        compiler_params=_cparams(("arbitrary", "arbitrary")),
        name="dsa_select",
    )(q3, wi, k3)


def _kmax(qi, tq, bk):
    return ((qi + 1) * tq - 1) // bk


def _softmax_update_t(s_t, m_ref, acc_ref, v_t, idx):
    m_old = m_ref[idx]
    m_new = jnp.maximum(m_old, jnp.max(s_t, axis=0, keepdims=True))
    alpha = jnp.exp(m_old - m_new)
    p = jnp.exp(s_t - m_new)
    acc_ref[idx] = alpha * acc_ref[idx] + _dot(v_t, p.astype(BF16))
    m_ref[idx] = m_new


def _attn_init(m_ref, acc_ref):
    m_ref[...] = jnp.full_like(m_ref, NEG)
    acc_ref[...] = jnp.zeros_like(acc_ref)


def _dsa_attn_body(q_ref, k_ref, vt_ref, b_ref, o_ref, m_ref, acc_ref, *, tq, bk, nk):
    qi, ki = pl.program_id(1), pl.program_id(2)

    @pl.when(ki == 0)
    def _():
        _attn_init(m_ref, acc_ref)

    @pl.when(ki <= _kmax(qi, tq, bk))
    def _():
        bias_t = b_ref[...].astype(F32).T
        for h in range(N_HEADS):
            sl = slice(h * HEAD_DIM, (h + 1) * HEAD_DIM)
            s_t = _dot_nt(k_ref[:, sl], q_ref[:, sl]) + bias_t
            _softmax_update_t(s_t, m_ref, acc_ref, vt_ref[h], h)

    @pl.when(ki == nk - 1)
    def _():
        for h in range(N_HEADS):
            sl = slice(h * HEAD_DIM, (h + 1) * HEAD_DIM)
            o_ref[:, sl] = (acc_ref[h, 0:HEAD_DIM] / acc_ref[h, HEAD_DIM:HEAD_DIM + 1]).T.astype(BF16)


def _kv_specs(tq, bk, d):
    k_spec = pl.BlockSpec((None, bk, d), lambda bb, i, j: (bb, jnp.minimum(j, _kmax(i, tq, bk)), 0))
    vt_spec = pl.BlockSpec((None, N_HEADS, VT_ROWS, bk),
                           lambda bb, i, j: (bb, 0, 0, jnp.minimum(j, _kmax(i, tq, bk))))
    return k_spec, vt_spec


def dsa_attention(q, k, v_t, bias, *, tq):
    b, t_len, d = q.shape
    bk = bias.shape[3]
    nk = t_len // bk
    k_spec, vt_spec = _kv_specs(tq, bk, d)
    return pl.pallas_call(
        functools.partial(_dsa_attn_body, tq=tq, bk=bk, nk=nk),
        grid=(b, t_len // tq, nk),
        in_specs=[
            pl.BlockSpec((None, tq, d), lambda bb, i, j: (bb, i, 0)),
            k_spec, vt_spec,
            pl.BlockSpec((None, None, tq, bk), lambda bb, i, j: (bb, jnp.minimum(j, _kmax(i, tq, bk)), i, 0)),
        ],
        out_specs=pl.BlockSpec((None, tq, d), lambda bb, i, j: (bb, i, 0)),
        out_shape=jax.ShapeDtypeStruct((b, t_len, d), BF16),
        scratch_shapes=[pltpu.VMEM((N_HEADS, 1, tq), F32), pltpu.VMEM((N_HEADS, VT_ROWS, tq), F32)],
        compiler_params=_cparams(("arbitrary", "arbitrary", "arbitrary")),
        name="dsa_attention",
    )(q, k, v_t, bias)


def _diff_lambda(lam_ref, lam_init):
    lp = lam_ref[...]
    a = jnp.sum(lp[0:1] * lp[1:2], axis=-1, keepdims=True)
    b = jnp.sum(lp[2:3] * lp[3:4], axis=-1, keepdims=True)
    return jnp.exp(a) - jnp.exp(b) + lam_init


def _diff_attn_body(q2_ref, k_ref, vt_ref, lam_ref, sub_ref, o_ref, m_ref, acc_ref,
                    *, tq, bk, nk, lam_init):
    qi, ki = pl.program_id(1), pl.program_id(2)
    kmax = _kmax(qi, tq, bk)

    @pl.when(ki == 0)
    def _():
        _attn_init(m_ref, acc_ref)

    def step(masked):
        q2 = q2_ref[...].reshape(2 * tq, N_HEADS * HEAD_DIM)
        if masked:
            keys = ki * bk + lax.broadcasted_iota(I32, (bk, 2 * tq), 0)
            queries = qi * tq + lax.broadcasted_iota(I32, (bk, 2 * tq), 1) % tq
            allowed = keys <= queries
        for h in range(N_HEADS):
            sl = slice(h * HEAD_DIM, (h + 1) * HEAD_DIM)
            s_t = _dot_nt(k_ref[:, sl], q2[:, sl])
            if masked:
                s_t = jnp.where(allowed, s_t, NEG)
            _softmax_update_t(s_t, m_ref, acc_ref, vt_ref[h], h)

    @pl.when((ki + 1) * bk - 1 <= qi * tq)
    def _():
        step(False)

    @pl.when(jnp.logical_and((ki + 1) * bk - 1 > qi * tq, ki <= kmax))
    def _():
        step(True)

    @pl.when(ki == nk - 1)
    def _():
        lam = _diff_lambda(lam_ref, lam_init)
        for h in range(N_HEADS):
            sl = slice(h * HEAD_DIM, (h + 1) * HEAD_DIM)
            o_all = acc_ref[h, 0:HEAD_DIM] / acc_ref[h, HEAD_DIM:HEAD_DIM + 1]
            o = o_all[:, 0:tq] - lam * o_all[:, tq:2 * tq]
            ms = jnp.mean(o * o, axis=0, keepdims=True)
            o = o * lax.rsqrt(ms + NORM_EPS) * sub_ref[...] * (1.0 - lam_init)
            o_ref[:, sl] = o.T.astype(BF16)


def diff_attention(q2, k, v_t, lam_params, subln_col, lam_init, *, tq, bk):
    b, _, t_len, d = q2.shape
    nk = t_len // bk
    k_spec, vt_spec = _kv_specs(tq, bk, d)
    return pl.pallas_call(
        functools.partial(_diff_attn_body, tq=tq, bk=bk, nk=nk, lam_init=lam_init),
        grid=(b, t_len // tq, nk),
        in_specs=[
            pl.BlockSpec((None, 2, tq, d), lambda bb, i, j: (bb, 0, i, 0)),
            k_spec, vt_spec,
            pl.BlockSpec(lam_params.shape, lambda bb, i, j: (0, 0)),
            pl.BlockSpec((HEAD_DIM, 1), lambda bb, i, j: (0, 0)),
        ],
        out_specs=pl.BlockSpec((None, tq, d), lambda bb, i, j: (bb, i, 0)),
        out_shape=jax.ShapeDtypeStruct((b, t_len, d), BF16),
        scratch_shapes=[pltpu.VMEM((N_HEADS, 1, 2 * tq), F32), pltpu.VMEM((N_HEADS, VT_ROWS, 2 * tq), F32)],
        compiler_params=_cparams(("arbitrary", "arbitrary", "arbitrary")),
        name="diff_attention",
    )(q2, k, v_t, lam_params, subln_col)


def _softplus(z):
    return jnp.maximum(z, 0.0) + jnp.log1p(jnp.exp(-jnp.abs(z)))


def _sb_attn_body(q_ref, k_ref, v_ref, u_ref, o_ref, c_ref, acc_ref, done_ref, *, tq, bk, sub, nk):
    qi, ki = pl.program_id(1), pl.program_id(2)
    kb = _kmax(qi, tq, bk) - ki

    @pl.when(ki == 0)
    def _():
        c_ref[...] = jnp.zeros_like(c_ref)
        acc_ref[...] = jnp.zeros_like(acc_ref)
        done_ref[0] = 0

    @pl.when(jnp.logical_and(kb >= 0, done_ref[0] == 0))
    def _():
        rows = qi * tq + lax.broadcasted_iota(I32, (tq, sub), 0)
        lane_cols = lax.broadcasted_iota(I32, (tq, sub), 1)
        u = u_ref[...]
        for h in range(N_HEADS):
            sl = slice(h * HEAD_DIM, (h + 1) * HEAD_DIM)
            crun = c_ref[h]
            q = q_ref[:, sl]
            for j in reversed(range(bk // sub)):
                ks = slice(j * sub, (j + 1) * sub)
                z = _dot_nt(q, k_ref[ks, sl])
                mask = kb * bk + j * sub + lane_cols < rows
                sp = _softplus(z)
                lk = jnp.where(mask, -sp, 0.0)
                lk_hi, lk_lo = _split_bf16(lk)
                after = crun + _dot(lk_hi, u) + _dot(lk_lo, u)
                a = jnp.where(mask, jnp.exp(after + z - sp), 0.0)
                acc_ref[:, sl] += _dot(a.astype(BF16), v_ref[ks, sl])
                crun = crun + jnp.sum(lk, axis=-1, keepdims=True)
            c_ref[h] = crun
        done_ref[0] = jnp.where(jnp.max(c_ref[...]) < SB_EXP_UNDERFLOW, 1, 0)

    @pl.when(ki == nk - 1)
    def _():
        o_ref[...] = acc_ref[...].astype(BF16)


def _later_key_matrix(n):
    j = lax.broadcasted_iota(I32, (n, n), 0)
    s = lax.broadcasted_iota(I32, (n, n), 1)
    return (j > s).astype(BF16)


def sb_attention(q, k, v, *, tq, bk, sub):
    b, t_len, d = q.shape
    nk = t_len // bk
    kv_spec = pl.BlockSpec((None, bk, d), lambda bb, i, j: (bb, jnp.maximum(_kmax(i, tq, bk) - j, 0), 0))
    return pl.pallas_call(
        functools.partial(_sb_attn_body, tq=tq, bk=bk, sub=sub, nk=nk),
        grid=(b, t_len // tq, nk),
        in_specs=[
            pl.BlockSpec((None, tq, d), lambda bb, i, j: (bb, i, 0)),
            kv_spec, kv_spec,
            pl.BlockSpec((sub, sub), lambda bb, i, j: (0, 0)),
        ],
        out_specs=pl.BlockSpec((None, tq, d), lambda bb, i, j: (bb, i, 0)),
        out_shape=jax.ShapeDtypeStruct((b, t_len, d), BF16),
        scratch_shapes=[pltpu.VMEM((N_HEADS, tq, 1), F32), pltpu.VMEM((tq, d), F32), pltpu.SMEM((1,), I32)],
        compiler_params=_cparams(("arbitrary", "arbitrary", "arbitrary")),
        name="sb_attention",
    )(q, k, v, _later_key_matrix(sub))


def _sample_scores_body(pt_ref, qh_ref, ql_ref, wi_ref, kn_ref, *rest, n_pages):
    page_refs, out_ref = rest[:n_pages], rest[n_pages]
    qh, ql = qh_ref[...], ql_ref[...]
    w = wi_ref[...]
    for p in range(n_pages):
        k_hi, k_lo = _split_bf16(page_refs[p][...])
        s = _dot_nt(qh, k_hi) + _dot_nt(ql, k_hi) + _dot_nt(qh, k_lo)
        out_ref[p:p + 1, :] = jnp.sum(jnp.maximum(s, 0.0) * w, axis=0, keepdims=True)
    q = qh.astype(F32) + ql.astype(F32)
    s_new = jnp.sum(q * kn_ref[...], axis=-1, keepdims=True)
    i_new = jnp.sum(jnp.maximum(s_new, 0.0) * w, axis=0, keepdims=True)
    lane = lax.broadcasted_iota(I32, (1, LANES), 1)
    out_ref[n_pages:n_pages + 1, :] = jnp.where(lane == 0, i_new, -jnp.inf)


def sample_scores(page_table, qh, ql, wi_col, ki_new, pool_kidx, layer):
    db, n_pages = page_table.shape
    page_specs = [
        pl.BlockSpec((None, None, PAGE_SIZE, IDX_DIM),
                     functools.partial(lambda r, pt, p: (layer, pt[r, p], 0, 0), p=p))
        for p in range(n_pages)]
    grid_spec = pltpu.PrefetchScalarGridSpec(
        num_scalar_prefetch=1,
        grid=(db,),
        in_specs=[
            pl.BlockSpec((None, IDX_HEADS, IDX_DIM), lambda r, pt: (r, 0, 0)),
            pl.BlockSpec((None, IDX_HEADS, IDX_DIM), lambda r, pt: (r, 0, 0)),
            pl.BlockSpec((None, IDX_HEADS, 1), lambda r, pt: (r, 0, 0)),
            pl.BlockSpec((None, 1, IDX_DIM), lambda r, pt: (r, 0, 0)),
        ] + page_specs,
        out_specs=pl.BlockSpec((None, n_pages + 1, LANES), lambda r, pt: (r, 0, 0)),
    )
    return pl.pallas_call(
        functools.partial(_sample_scores_body, n_pages=n_pages),
        grid_spec=grid_spec,
        out_shape=jax.ShapeDtypeStruct((db, n_pages + 1, LANES), F32),
        compiler_params=_cparams(("arbitrary",)),
        name="sample_scores",
    )(page_table, qh, ql, wi_col, ki_new, *([pool_kidx] * n_pages))


def _sample_select_body(sc_ref, e_ref, bias_ref, *, db, n_chunks, ktop):
    lane = lax.broadcasted_iota(I32, (db, LANES), 1)
    keys = []
    for c in range(n_chunks):
        s = sc_ref[:, c * LANES:(c + 1) * LANES]
        keys.append(jnp.where(s == -jnp.inf, INT_MIN, _sort_key(s)))

    def count(ind):
        cnt = jnp.zeros((db, LANES), I32)
        for c in range(n_chunks):
            cnt = cnt + ind(keys[c], c * LANES + lane)
        return _lane_total(cnt)

    thr = _kth_largest(lambda trial: count(lambda kc, cols: jnp.where(kc >= trial, 1, 0)), db, ktop)
    need = ktop - count(lambda kc, cols: jnp.where(kc > thr, 1, 0))
    nbits = max(1, (n_chunks * LANES - 1).bit_length())
    cut = _tie_cutoff(
        lambda trial: count(lambda kc, cols: jnp.where(kc == thr, jnp.where(cols < trial, 1, 0), 0)),
        need, db, nbits)
    e = e_ref[...]
    for c in range(n_chunks):
        cols = c * LANES + lane
        tie = jnp.where(keys[c] == thr, jnp.where(cols <= cut, 1.0, 0.0), 0.0)
        sel = jnp.where(keys[c] > thr, 1.0, tie)
        sel = jnp.where(keys[c] == INT_MIN, 0.0, sel)
        wide = _dot(sel.astype(BF16), e)
        bias_ref[c] = jnp.where(wide > 0.5, 0.0, NEG)


def sample_select(scores2d, ktop):
    db, n = scores2d.shape
    n_chunks = n // LANES
    flat = N_HEADS * PAGE_SIZE
    t_of_lane = lax.broadcasted_iota(I32, (PAGE_SIZE, flat), 1) // N_HEADS
    e = (t_of_lane == lax.broadcasted_iota(I32, (PAGE_SIZE, flat), 0)).astype(BF16)
    return pl.pallas_call(
        functools.partial(_sample_select_body, db=db, n_chunks=n_chunks, ktop=ktop),
        grid=(1,),
        in_specs=[pl.BlockSpec((db, n), lambda i: (0, 0)), pl.BlockSpec((PAGE_SIZE, flat), lambda i: (0, 0))],
        out_specs=pl.BlockSpec((n_chunks, db, flat), lambda i: (0, 0, 0)),
        out_shape=jax.ShapeDtypeStruct((n_chunks, db, flat), F32),
        compiler_params=_cparams(("arbitrary",)),
        name="sample_select",
    )(scores2d, e)


def _suffix_sum_stride8(x):
    n = x.shape[1]
    lane = lax.broadcasted_iota(I32, x.shape, 1)
    sh = N_HEADS
    while sh < n:
        x = x + jnp.where(lane < n - sh, pltpu.roll(x, n - sh, 1), 0.0)
        sh *= 2
    return x


def _sample_attn_body(pt_ref, q_ref, kn_ref, vn_ref, *rest, mode, pps, n_steps, lam_init):
    k_refs, v_refs = rest[:pps], rest[pps:2 * pps]
    rest = rest[2 * pps:]
    if mode == "dsa":
        bias_ref, bnew_ref, o_ref, m_ref, l_ref, acc_ref = rest
    elif mode == "diff":
        lam_ref, sub_ref, o_ref, m_ref, l_ref, acc_ref = rest
    else:
        o_ref, m_ref, l_ref, acc_ref = rest
    s_id = pl.program_id(1)
    nrow = q_ref.shape[0]
    flat = N_HEADS * PAGE_SIZE
    q = q_ref[...]
    valid = (lax.broadcasted_iota(I32, (nrow, flat), 1) % N_HEADS
             == lax.broadcasted_iota(I32, (nrow, flat), 0) % N_HEADS)

    @pl.when(s_id == 0)
    def _():
        m_ref[...] = jnp.full_like(m_ref, 0.0 if mode == "sb" else NEG)
        l_ref[...] = jnp.zeros_like(l_ref)
        acc_ref[...] = jnp.zeros_like(acc_ref)

    def k2d(i):
        return k_refs[i][...].reshape(flat, HEAD_DIM).astype(BF16)

    def weighted_values(weights):
        out = None
        for i in range(pps):
            pv = _dot(weights[i].astype(BF16), v_refs[i][...].reshape(flat, HEAD_DIM).astype(BF16))
            out = pv if out is None else out + pv
        return out

    if mode == "sb":
        z = [_dot_nt(q, k2d(i)) for i in range(pps)]
        sp = [_softplus(z[i]) for i in range(pps)]
        lk = [jnp.where(valid, -sp[i], 0.0) for i in range(pps)]
        run = m_ref[...]
        a = [None] * pps
        for i in reversed(range(pps)):
            after = run + (_suffix_sum_stride8(lk[i]) - lk[i])
            a[i] = jnp.where(valid, jnp.exp(after + z[i] - sp[i]), 0.0)
            run = run + jnp.sum(lk[i], axis=-1, keepdims=True)
        acc_ref[...] += weighted_values(a)
        m_ref[...] = run
    else:
        s = []
        for i in range(pps):
            s_i = jnp.where(valid, _dot_nt(q, k2d(i)), NEG)
            if mode == "dsa":
                s_i = s_i + bias_ref[i:i + 1, :]
            s.append(s_i)
        m_old = m_ref[...]
        s_max = functools.reduce(jnp.maximum, s)
        m_new = jnp.maximum(m_old, jnp.max(s_max, axis=-1, keepdims=True))
        alpha = jnp.exp(m_old - m_new)
        p = [jnp.where(valid, jnp.exp(s[i] - m_new), 0.0) for i in range(pps)]
        p_sum = functools.reduce(jnp.add, p)
        l_ref[...] = alpha * l_ref[...] + jnp.sum(p_sum, axis=-1, keepdims=True)
        acc_ref[...] = alpha * acc_ref[...] + weighted_values(p)
        m_ref[...] = m_new

    @pl.when(s_id == n_steps - 1)
    def _():
        reps = nrow // N_HEADS
        k_new = jnp.concatenate([kn_ref[...]] * reps, axis=0)
        v_new = jnp.concatenate([vn_ref[...]] * reps, axis=0)
        if mode == "sb":
            o_ref[...] = acc_ref[...] + jnp.zeros((nrow, 1), F32) * v_new
            return
        s_new = jnp.sum(q.astype(F32) * k_new, axis=-1, keepdims=True)
        if mode == "dsa":
            s_new = s_new + bnew_ref[:, 0:1]
        m_old = m_ref[...]
        m_new = jnp.maximum(m_old, s_new)
        alpha = jnp.exp(m_old - m_new)
        p_new = jnp.exp(s_new - m_new)
        l_fin = alpha * l_ref[...] + p_new
        o = (alpha * acc_ref[...] + p_new * v_new) / l_fin
        if mode == "diff":
            lam = _diff_lambda(lam_ref, lam_init)
            o = o[0:N_HEADS] - lam * o[N_HEADS:2 * N_HEADS]
            ms = jnp.mean(o * o, axis=-1, keepdims=True)
            o = o * lax.rsqrt(ms + NORM_EPS) * sub_ref[...] * (1.0 - lam_init)
        o_ref[...] = o


def sample_attention(mode, page_table, q_rows, k_new, v_new, pool_k, pool_v, layer, *, pps,
                     bias_pages=None, bias_new=None, lam_params=None, subln=None, lam_init=0.0):
    db, n_pages = page_table.shape
    nrow = q_rows.shape[1]
    n_steps = n_pages // pps
    flat = N_HEADS * PAGE_SIZE

    def page_map(i):
        if mode == "sb":
            return lambda r, s, pt: (layer, pt[r, (n_steps - 1 - s) * pps + i], 0, 0, 0)
        return lambda r, s, pt: (layer, pt[r, s * pps + i], 0, 0, 0)

    page_specs = [pl.BlockSpec((None, None, PAGE_SIZE, N_HEADS, HEAD_DIM), page_map(i)) for i in range(pps)]
    in_specs = [
        pl.BlockSpec((None, nrow, HEAD_DIM), lambda r, s, pt: (r, 0, 0)),
        pl.BlockSpec((None, N_HEADS, HEAD_DIM), lambda r, s, pt: (r, 0, 0)),
        pl.BlockSpec((None, N_HEADS, HEAD_DIM), lambda r, s, pt: (r, 0, 0)),
    ] + page_specs + page_specs
    args = [q_rows, k_new, v_new] + [pool_k] * pps + [pool_v] * pps
    if mode == "dsa":
        in_specs += [pl.BlockSpec((None, None, pps, flat), lambda r, s, pt: (r, s, 0, 0)),
                     pl.BlockSpec((None, 1, flat), lambda r, s, pt: (r, 0, 0))]
        args += [bias_pages, bias_new]
    elif mode == "diff":
        in_specs += [pl.BlockSpec(lam_params.shape, lambda r, s, pt: (0, 0)),
                     pl.BlockSpec((1, HEAD_DIM), lambda r, s, pt: (0, 0))]
        args += [lam_params, subln]
    grid_spec = pltpu.PrefetchScalarGridSpec(
        num_scalar_prefetch=1,
        grid=(db, n_steps),
        in_specs=in_specs,
        out_specs=pl.BlockSpec((None, N_HEADS, HEAD_DIM), lambda r, s, pt: (r, 0, 0)),
        scratch_shapes=[pltpu.VMEM((nrow, 1), F32), pltpu.VMEM((nrow, 1), F32),
                        pltpu.VMEM((nrow, HEAD_DIM), F32)],
    )
    return pl.pallas_call(
        functools.partial(_sample_attn_body, mode=mode, pps=pps, n_steps=n_steps, lam_init=lam_init),
        grid_spec=grid_spec,
        out_shape=jax.ShapeDtypeStruct((db, N_HEADS, HEAD_DIM), F32),
        compiler_params=_cparams(("arbitrary", "arbitrary")),
        name="sample_attention_" + mode,
    )(page_table, *args)


def _idx_weight(w_in):
    d = w_in.shape[0]
    hd = N_HEADS * HEAD_DIM
    base = 3 * hd
    cols = []
    zeros64 = jnp.zeros((d, LANES - IDX_DIM), w_in.dtype)
    for h in range(IDX_HEADS):
        cols += [w_in[:, base + h * IDX_DIM:base + (h + 1) * IDX_DIM], zeros64]
    cols += [w_in[:, base + IDX_HEADS * IDX_DIM:base + IDX_HEADS * IDX_DIM + IDX_DIM], zeros64]
    wi = w_in[:, base + IDX_HEADS * IDX_DIM + IDX_DIM:]
    cols += [wi, jnp.zeros((d, LANES - wi.shape[1]), w_in.dtype)]
    return jnp.concatenate(cols, axis=1)


def kernel(x_prompt, x_sample, cache_a_k, cache_a_v, cache_a_kidx, cache_b_k, cache_b_v, cache_c_k, cache_c_v, page_table, c_prompt, c_sample, ada_w, ada_b, norm_mix, norm_ffn, ffn_w1, ffn_w2, a_w_in, a_q_norm, a_k_norm, a_idx_k_norm, a_w_out, b_w_in, b_q_norm, b_k_norm, b_lambda, b_subln, b_w_out, c_w_in, c_w_out):
    b, t_len, d = x_prompt.shape
    db = x_sample.shape[0]
    depth = ada_w.shape[0]
    n_pages = page_table.shape[1]
    past = n_pages * PAGE_SIZE
    hd = (N_HEADS, HEAD_DIM)

    n_c = b + db
    n_c_pad = -(-n_c // 8) * 8
    c_all = jnp.concatenate([c_prompt, c_sample, jnp.zeros((n_c_pad - n_c, d), F32)], axis=0)
    mod = ada_modulation_all(c_all, ada_w, ada_b)

    tabs128_p = rope_tables(0, t_len, 64)
    tabs64_p = rope_tables(0, t_len, 32)
    tabs128_s = tuple(x[0:1] for x in rope_tables(past, 8, 64))
    tabs64_s = tuple(x[0:1] for x in rope_tables(past, 8, 32))

    xp = x_prompt
    xs = x_sample.reshape(1, db, d)
    rows = {name: [] for name in ("a_k", "a_v", "a_kidx", "b_k", "b_v", "c_k", "c_v")}
    vec = lambda a: a.reshape(1, -1)

    for i in range(depth):
        kind, j = i % N_MIXERS, i // N_MIXERS
        mp = [mod[i, :b, k * d:(k + 1) * d].reshape(b, 1, d) for k in range(6)]
        ms = [mod[i, b:n_c, k * d:(k + 1) * d].reshape(1, db, d) for k in range(6)]
        g_mix, g_ffn = vec(norm_mix[i]), vec(norm_ffn[i])

        if kind == 0:
            w_in = a_w_in[j]
            w_qkv = (w_in[:, :3 * d].astype(BF16),)
            w_idx = _split_bf16(_idx_weight(w_in))
            qn, kn = vec(a_q_norm[j]), vec(a_k_norm[j])
            ikn = jnp.pad(a_idx_k_norm[j], (0, LANES - IDX_DIM)).reshape(1, LANES)
            w_out = a_w_out[j].astype(BF16)

            raw = proj(xp, g_mix, mp[0], mp[1], w_qkv, tm=1024, tn=1024, name="proj_dsa")
            idx = proj(xp, g_mix, mp[0], mp[1], w_idx, tm=512, tn=w_idx[0].shape[1], name="proj_dsa_idx")
            q_bf, k_f, k_bf, v_t, q3, ki_f, k3, wi = post_a(raw, idx, tabs128_p, tabs64_p, qn, kn, ikn, tm=256)
            bias = dsa_select(q3, wi, k3, tq=256, ck=512)
            op = dsa_attention(q_bf, k_bf, v_t, bias, tq=min(1024, t_len))
            kp, vp, kip = k_f.reshape(b, t_len, *hd), raw[:, :, 2 * d:].reshape(b, t_len, *hd), ki_f

            raw_s = proj(xs, g_mix, ms[0], ms[1], w_qkv, tm=128, tn=1024, name="proj_dsa")
            idx_s = proj(xs, g_mix, ms[0], ms[1], w_idx, tm=128, tn=w_idx[0].shape[1], name="proj_dsa_idx")
            qs_bf, ks_f, _, _, q3_s, kis_f, _, wi_s = post_a(raw_s, idx_s, tabs128_s, tabs64_s, qn, kn, ikn, tm=128)
            qh = jnp.transpose(q3_s[0, :, :, 0:IDX_DIM], (1, 0, 2))
            ql = jnp.transpose(q3_s[0, :, :, IDX_DIM:2 * IDX_DIM], (1, 0, 2))
            scores = sample_scores(page_table, qh, ql, wi_s.reshape(db, IDX_HEADS, 1),
                                   kis_f.reshape(db, 1, IDX_DIM), cache_a_kidx, j)
            ktop = min(TOPK_MAX, (past + 1) // 4)
            sel_bias = sample_select(scores.reshape(db, (n_pages + 1) * LANES), ktop)
            pps = 8
            bias_pages = jnp.transpose(sel_bias[:n_pages], (1, 0, 2)).reshape(db, n_pages // pps, pps, -1)
            bias_new = jnp.transpose(sel_bias[n_pages:], (1, 0, 2))
            ks_, vs_ = ks_f.reshape(db, *hd), raw_s[0, :, 2 * d:].reshape(db, *hd)
            os_ = sample_attention("dsa", page_table, qs_bf.reshape(db, *hd), ks_, vs_, cache_a_k, cache_a_v, j,
                                   pps=pps, bias_pages=bias_pages, bias_new=bias_new)
            rows["a_k"].append((kp, ks_))
            rows["a_v"].append((vp, vs_))
            rows["a_kidx"].append((kip, kis_f.reshape(db, 1, IDX_DIM)))
        elif kind == 1:
            lam_init = 0.8 - 0.6 * math.exp(-0.3 * i)
            w_in = (b_w_in[j].astype(BF16),)
            qn = vec(jnp.tile(b_q_norm[j], 2))
            kn = vec(jnp.tile(b_k_norm[j], 2))
            subln = vec(b_subln[j])
            w_out = b_w_out[j].astype(BF16)

            raw = proj(xp, g_mix, mp[0], mp[1], w_in, tm=1024, tn=1024, name="proj_diff")
            q2, k_f, k_bf, v_t = post_b(raw, tabs64_p, qn, kn, tm=256)
            op = diff_attention(q2, k_bf, v_t, b_lambda[j], subln.reshape(HEAD_DIM, 1), lam_init, tq=512,
                                bk=min(1024, t_len))
            kp, vp = k_f.reshape(b, t_len, *hd), raw[:, :, 2 * d:].reshape(b, t_len, *hd)

            raw_s = proj(xs, g_mix, ms[0], ms[1], w_in, tm=128, tn=1024, name="proj_diff")
            q2_s, ks_f, _, _ = post_b(raw_s, tabs64_s, qn, kn, tm=128)
            q_rows = jnp.transpose(q2_s[0].reshape(2, db, *hd), (1, 0, 2, 3)).reshape(db, 2 * N_HEADS, HEAD_DIM)
            ks_, vs_ = ks_f.reshape(db, *hd), raw_s[0, :, 2 * d:].reshape(db, *hd)
            os_ = sample_attention("diff", page_table, q_rows, ks_, vs_, cache_b_k, cache_b_v, j, pps=8,
                                   lam_params=b_lambda[j], subln=subln, lam_init=lam_init)
            rows["b_k"].append((kp, ks_))
            rows["b_v"].append((vp, vs_))
        else:
            w_in = (c_w_in[j].astype(BF16),)
            w_out = c_w_out[j].astype(BF16)

            raw = proj(xp, g_mix, mp[0], mp[1], w_in, tm=1024, tn=1024, name="proj_sb")
            q_bf, k_bf, v_bf = post_c(raw, tm=256)
            op = sb_attention(q_bf, k_bf, v_bf, tq=512, bk=512, sub=256)
            kp, vp = raw[:, :, d:2 * d].reshape(b, t_len, *hd), raw[:, :, 2 * d:].reshape(b, t_len, *hd)

            raw_s = proj(xs, g_mix, ms[0], ms[1], w_in, tm=128, tn=1024, name="proj_sb")
            qs_bf, _, _ = post_c(raw_s, tm=128)
            ks_, vs_ = raw_s[0, :, d:2 * d].reshape(db, *hd), raw_s[0, :, 2 * d:].reshape(db, *hd)
            os_ = sample_attention("sb", page_table, qs_bf.reshape(db, *hd), ks_, vs_, cache_c_k, cache_c_v, j, pps=8)
            rows["c_k"].append((kp, ks_))
            rows["c_v"].append((vp, vs_))

        w1, w2 = ffn_w1[i].astype(BF16), ffn_w2[i].astype(BF16)
        xp = out_proj_residual(op, w_out, xp, mp[2], tm=1024)
        xp = mlp_residual(xp, g_ffn, mp[3], mp[4], mp[5], w1, w2, tm=1024, tf=1024)
        xs = out_proj_residual(os_.reshape(1, db, d).astype(BF16), w_out, xs, ms[2], tm=128)
        xs = mlp_residual(xs, g_ffn, ms[3], ms[4], ms[5], w1, w2, tm=128, tf=1024)

    def stack(name, which):
        return jnp.stack([r[which] for r in rows[name]])

    outs = [xp, xs.reshape(db, 1, d)]
    for which in (0, 1):
        for name in ("a_k", "a_v", "a_kidx", "b_k", "b_v", "c_k", "c_v"):
            y = stack(name, which)
            if which == 1 and name != "a_kidx":
                y = y.reshape(y.shape[0], db, 1, *hd)
            outs.append(y)
    return tuple(outs)
```
